```python
import math
import jax, jax.numpy as jnp
from jax import lax
import numpy as np

D_MODEL = 1024
BATCH = 16
SEQ = 2048
DEPTH = 2
DEC_BATCH = 128
DEC_SEQ = 1
PAST_LEN = 8192
PAGE_SIZE = 128

A_HEADS = 4
A_HEAD_DIM = 64
A_WIDTH = A_HEADS * A_HEAD_DIM
W_LORA = 64
A_LORA = 64
G_LORA = 128
RWKV_COLS = 3 * A_WIDTH + W_LORA + A_LORA + G_LORA
GN_EPS = 64e-5
B_HEADS = 8
Q_RANK = 256
KV_RANK = 128
NOPE_DIM = 64
ROPE_DIM = 32
V_DIM = 64
B_WIDTH = B_HEADS * V_DIM
MLA_COLS = Q_RANK + KV_RANK + ROPE_DIM
ROPE_BASE = 10000.0
C_HEADS = 4
C_HEAD_DIM = 64
C_WIDTH = C_HEADS * C_HEAD_DIM
SB_COLS = 3 * C_WIDTH
IN_COLS = 3 * D_MODEL + RWKV_COLS + MLA_COLS + SB_COLS
FFN_DIM = 2816
Q_BLOCK = 128
EPS = 1e-6

kernel_name = "hybrid_rwkv7_mla_stickbreak_step"


def _split(p, sizes):
    out, o = [], 0
    for s in sizes:
        out.append(p[..., o:o + s])
        o += s
    return out


def rms_norm(x, g):
    xf = x.astype(jnp.float32)
    y = xf * lax.rsqrt(jnp.mean(xf * xf, axis=-1, keepdims=True) + EPS)
    return (y * g.astype(jnp.float32)).astype(x.dtype)


def swiglu_ffn(x, w_in, w_out):
    gt, up = jnp.split(x @ w_in, 2, axis=-1)
    return (jax.nn.silu(gt) * up) @ w_out


def rope(x, pos):
    half = ROPE_DIM // 2
    inv = ROPE_BASE ** (-jnp.arange(half, dtype=jnp.float32) / half)
    ang = pos.astype(jnp.float32)[:, None] * inv[None, :]
    c = jnp.cos(ang)[:, None, :].astype(x.dtype)
    s = jnp.sin(ang)[:, None, :].astype(x.dtype)
    x1, x2 = x[..., :half], x[..., half:]
    return jnp.concatenate([x1 * c - x2 * s, x1 * s + x2 * c], axis=-1)


def _rwkv7_step(S, inp):
    r_t, w_t, k_t, v_t, kk_t, b_t = inp
    sa = jnp.einsum('bhij,bhj->bhi', S, -kk_t)
    S = S * w_t[:, :, None, :] + sa[..., None] * b_t[:, :, None, :] + v_t[..., None] * k_t[:, :, None, :]
    return S, jnp.einsum('bhij,bhj->bhi', S, r_t)


def rwkv7_branch(p, shift_prev, state0, mu, w0, w_up, a0, a_up, g_up, k_k, k_a, r_k, ln_w, ln_b):
    f32 = jnp.float32
    B, T, _ = p.shape
    p_prev = jnp.concatenate([shift_prev[:, None, :].astype(p.dtype), p[:, :-1]], axis=1)
    xm = p + (p_prev - p) * mu
    r, k, v, wd, ad, gd = _split(xm, (A_WIDTH, A_WIDTH, A_WIDTH, W_LORA, A_LORA, G_LORA))
    w_raw = -jax.nn.softplus(-(w0 + jnp.tanh(wd) @ w_up)) - 0.5
    decay = jnp.exp(-jnp.exp(w_raw.astype(f32)))
    a = jax.nn.sigmoid(a0 + ad @ a_up)
    g = jax.nn.sigmoid(gd) @ g_up
    kk = (k * k_k).reshape(B, T, A_HEADS, A_HEAD_DIM).astype(f32)
    kk = kk / jnp.maximum(jnp.linalg.norm(kk, axis=-1, keepdims=True), 1e-12)
    k = k * (1.0 + (a - 1.0) * k_a)
    r_h, k_h, v_h, a_h, w_h = [t.reshape(B, T, A_HEADS, A_HEAD_DIM).astype(f32)
                               for t in (r, k, v, a, decay)]
    xs = tuple(jnp.swapaxes(t, 0, 1) for t in (r_h, w_h, k_h, v_h, kk, kk * a_h))
    S_final, y = lax.scan(_rwkv7_step, state0.astype(f32), xs)
    y = jnp.swapaxes(y, 0, 1)
    mean = jnp.mean(y, axis=-1, keepdims=True)
    var = jnp.mean(jnp.square(y - mean), axis=-1, keepdims=True)
    yn = (y - mean) * lax.rsqrt(var + GN_EPS)
    yn = yn * ln_w.reshape(A_HEADS, A_HEAD_DIM).astype(f32) + ln_b.reshape(A_HEADS, A_HEAD_DIM).astype(f32)
    bonus = jnp.sum(r_h * k_h * r_k.astype(f32), axis=-1, keepdims=True) * v_h
    out = ((yn + bonus).reshape(B, T, A_WIDTH) * g).astype(p.dtype)
    return out, S_final.astype(state0.dtype), p[:, -1]


def mla_branch(p, pos, past_lat, past_krope, q_norm, w_uq, kv_norm, w_uk, w_uv):
    B, T, _ = p.shape
    cq, ckv, kr = _split(p, (Q_RANK, KV_RANK, ROPE_DIM))
    q = (rms_norm(cq, q_norm) @ w_uq).reshape(B, T, B_HEADS, NOPE_DIM + ROPE_DIM)
    q_nope, q_rope = q[..., :NOPE_DIM], rope(q[..., NOPE_DIM:], pos)
    lat = rms_norm(ckv, kv_norm)
    k_rope = rope(kr[:, :, None, :], pos)[:, :, 0]
    scale = 1.0 / math.sqrt(NOPE_DIM + ROPE_DIM)
    w_uk_h = w_uk.reshape(KV_RANK, B_HEADS, NOPE_DIM)
    w_uv_h = w_uv.reshape(KV_RANK, B_HEADS, V_DIM)
    if past_lat is None:
        k_nope = jnp.einsum('btc,chd->bthd', lat, w_uk_h)
        v = jnp.einsum('btc,chd->bthd', lat, w_uv_h)
        kpos = jnp.arange(T)

        def block(i):
            q0 = i * Q_BLOCK
            qn = lax.dynamic_slice_in_dim(q_nope, q0, Q_BLOCK, axis=1)
            qr = lax.dynamic_slice_in_dim(q_rope, q0, Q_BLOCK, axis=1)
            s = (jnp.einsum('bqhd,bkhd->bhqk', qn, k_nope)
                 + jnp.einsum('bqhr,bkr->bhqk', qr, k_rope)).astype(jnp.float32) * scale
            qpos = q0 + jnp.arange(Q_BLOCK)
            s = jnp.where(kpos[None, :] <= qpos[:, None], s, -jnp.inf)
            pr = jax.nn.softmax(s, axis=-1).astype(v.dtype)
            return jnp.einsum('bhqk,bkhd->bqhd', pr, v)

        o = lax.map(block, jnp.arange(T // Q_BLOCK))
        o = jnp.moveaxis(o, 0, 1).reshape(B, T, B_WIDTH)
    else:
        P = past_lat.shape[1]
        q_lat = jnp.einsum('bthd,chd->bthc', q_nope, w_uk_h)
        s_past = (jnp.einsum('bthc,bsc->bhts', q_lat, past_lat)
                  + jnp.einsum('bthr,bsr->bhts', q_rope, past_krope)).astype(jnp.float32)
        s_new = (jnp.einsum('bthc,bsc->bhts', q_lat, lat)
                 + jnp.einsum('bthr,bsr->bhts', q_rope, k_rope)).astype(jnp.float32)
        tpos = jnp.arange(T)
        s_new = jnp.where(tpos[None, :] <= tpos[:, None], s_new, -jnp.inf)
        pr = jax.nn.softmax(jnp.concatenate([s_past, s_new], axis=-1) * scale, axis=-1).astype(lat.dtype)
        o_lat = (jnp.einsum('bhts,bsc->bthc', pr[..., :P], past_lat)
                 + jnp.einsum('bhts,bsc->bthc', pr[..., P:], lat))
        o = jnp.einsum('bthc,chd->bthd', o_lat, w_uv_h).reshape(B, T, B_WIDTH)
    return o, lat, k_rope


def stick_breaking_weights(z, mask):
    L = jnp.where(mask, jax.nn.softplus(z), 0.0)
    R = lax.cumsum(L, axis=z.ndim - 1, reverse=True) - L
    return jnp.where(mask, jnp.exp(jax.nn.log_sigmoid(z) - R), 0.0)


def sb_branch(p, past_k, past_v):
    B, T, _ = p.shape
    q, k, v = [t.reshape(B, T, C_HEADS, C_HEAD_DIM) for t in _split(p, (C_WIDTH,) * 3)]
    scale = C_HEAD_DIM ** -0.5
    if past_k is None:
        kpos = jnp.arange(T)

        def block(i):
            q0 = i * Q_BLOCK
            qb = lax.dynamic_slice_in_dim(q, q0, Q_BLOCK, axis=1)
            z = jnp.einsum('bqhd,bkhd->bhqk', qb, k).astype(jnp.float32) * scale
            qpos = q0 + jnp.arange(Q_BLOCK)
            A = stick_breaking_weights(z, kpos[None, :] < qpos[:, None]).astype(v.dtype)
            return jnp.einsum('bhqk,bkhd->bqhd', A, v)

        o = lax.map(block, jnp.arange(T // Q_BLOCK))
        o = jnp.moveaxis(o, 0, 1).reshape(B, T, C_WIDTH)
    else:
        P = past_k.shape[1]
        z = jnp.concatenate([jnp.einsum('bqhd,bkhd->bhqk', q, past_k).astype(jnp.float32),
                             jnp.einsum('bqhd,bkhd->bhqk', q, k).astype(jnp.float32)], axis=-1) * scale
        tpos = jnp.arange(T)
        mask = jnp.concatenate([jnp.ones((T, P), dtype=bool), tpos[None, :] < tpos[:, None]], axis=-1)
        A = stick_breaking_weights(z, mask).astype(v.dtype)
        o = (jnp.einsum('bhqk,bkhd->bqhd', A[..., :P], past_v)
             + jnp.einsum('bhqk,bkhd->bqhd', A[..., P:], v)).reshape(B, T, C_WIDTH)
    return o, k, v


def _layer(x, pos, past, rwkv_state, rwkv_shift, W):
    (n1, f1_in, f1_out, nm, w_in, rwkv_w, mla_w, wba, wbb, wbc, wo, n2, f2_in, f2_out) = W
    lat_past, kr_past, k_past, v_past = past
    x = x + 0.5 * swiglu_ffn(rms_norm(x, n1), f1_in, f1_out)
    h = rms_norm(x, nm)
    proj = h @ w_in
    g_a, g_b, g_c, p_a, p_b, p_c = _split(proj, (D_MODEL, D_MODEL, D_MODEL, RWKV_COLS, MLA_COLS, SB_COLS))
    y_a, new_S, new_shift = rwkv7_branch(p_a, rwkv_shift, rwkv_state, *rwkv_w)
    y_b, lat, kr = mla_branch(p_b, pos, lat_past, kr_past, *mla_w)
    y_c, k, v = sb_branch(p_c, k_past, v_past)
    merged = (jax.nn.sigmoid(g_a) * (y_a @ wba) + jax.nn.sigmoid(g_b) * (y_b @ wbb)
              + jax.nn.sigmoid(g_c) * (y_c @ wbc))
    x = x + merged @ wo
    x = x + 0.5 * swiglu_ffn(rms_norm(x, n2), f2_in, f2_out)
    return x, (lat, kr, k, v, new_S, new_shift)


def setup_inputs(seed: int = 0) -> dict:
    key = jax.random.key(seed)
    ks = iter(jax.random.split(key, 64))
    f32 = jnp.float32

    def nrm(shape, scale):
        return scale * jax.random.normal(next(ks), shape, f32)

    def gain(shape):
        return 1.0 + nrm(shape, 0.05)

    n_pages = PAST_LEN // PAGE_SIZE
    n_pool = (DEC_BATCH * n_pages * 5) // 4
    perm = jax.random.permutation(next(ks), n_pool)
    page_table = perm[:DEC_BATCH * n_pages].reshape(DEC_BATCH, n_pages).astype(jnp.int32)
    L, D = DEPTH, D_MODEL
    w0_base = jnp.tile(jnp.linspace(-5.0, 1.0, A_HEAD_DIM, dtype=f32), A_HEADS)
    return {
        "x_prompt": nrm((BATCH, SEQ, D), 1.0),
        "x_sample": nrm((DEC_BATCH, DEC_SEQ, D), 1.0),
        "cache_mla_latent": nrm((L, n_pool, PAGE_SIZE, KV_RANK), 1.0),
        "cache_mla_krope": nrm((L, n_pool, PAGE_SIZE, ROPE_DIM), 1.0),
        "cache_sb_k": nrm((L, n_pool, PAGE_SIZE, C_HEADS, C_HEAD_DIM), 1.0),
        "cache_sb_v": nrm((L, n_pool, PAGE_SIZE, C_HEADS, C_HEAD_DIM), 1.0),
        "state_rwkv": nrm((L, DEC_BATCH, A_HEADS, A_HEAD_DIM, A_HEAD_DIM), 0.5),
        "state_rwkv_shift": nrm((L, DEC_BATCH, RWKV_COLS), 1.0),
        "page_table": page_table,
        "norm_ffn1": gain((L, D)),
        "ffn1_w_in": nrm((L, D, 2 * FFN_DIM), D ** -0.5),
        "ffn1_w_out": nrm((L, FFN_DIM, D), FFN_DIM ** -0.5),
        "norm_mix": gain((L, D)),
        "w_mix_in": nrm((L, D, IN_COLS), D ** -0.5),
        "rwkv_mu": jax.random.uniform(next(ks), (L, RWKV_COLS), f32),
        "rwkv_w0": w0_base[None, :] + nrm((L, A_WIDTH), 0.1),
        "rwkv_w_up": nrm((L, W_LORA, A_WIDTH), 0.5 * W_LORA ** -0.5),
        "rwkv_a0": nrm((L, A_WIDTH), 0.1),
        "rwkv_a_up": nrm((L, A_LORA, A_WIDTH), 0.5 * A_LORA ** -0.5),
        "rwkv_g_up": nrm((L, G_LORA, A_WIDTH), G_LORA ** -0.5),
        "rwkv_k_k": 0.85 + nrm((L, A_WIDTH), 0.02),
        "rwkv_k_a": 1.0 + nrm((L, A_WIDTH), 0.02),
        "rwkv_r_k": nrm((L, A_HEADS, A_HEAD_DIM), 0.1),
        "rwkv_ln_w": gain((L, A_WIDTH)),
        "rwkv_ln_b": nrm((L, A_WIDTH), 0.02),
        "mla_q_norm": gain((L, Q_RANK)),
        "mla_w_uq": nrm((L, Q_RANK, B_HEADS * (NOPE_DIM + ROPE_DIM)), Q_RANK ** -0.5),
        "mla_kv_norm": gain((L, KV_RANK)),
        "mla_w_uk": nrm((L, KV_RANK, B_HEADS * NOPE_DIM), KV_RANK ** -0.5),
        "mla_w_uv": nrm((L, KV_RANK, B_HEADS * V_DIM), KV_RANK ** -0.5),
        "w_branch_a": nrm((L, A_WIDTH, D), A_WIDTH ** -0.5),
        "w_branch_b": nrm((L, B_WIDTH, D), B_WIDTH ** -0.5),
        "w_branch_c": nrm((L, C_WIDTH, D), C_WIDTH ** -0.5),
        "w_mix_out": nrm((L, D, D), D ** -0.5),
        "norm_ffn2": gain((L, D)),
        "ffn2_w_in": nrm((L, D, 2 * FFN_DIM), D ** -0.5),
        "ffn2_w_out": nrm((L, FFN_DIM, D), FFN_DIM ** -0.5),
        "final_norm": gain((D,)),
    }


def reference(x_prompt, x_sample, cache_mla_latent, cache_mla_krope, cache_sb_k, cache_sb_v,
              state_rwkv, state_rwkv_shift, page_table,
              norm_ffn1, ffn1_w_in, ffn1_w_out, norm_mix, w_mix_in,
              rwkv_mu, rwkv_w0, rwkv_w_up, rwkv_a0, rwkv_a_up, rwkv_g_up, rwkv_k_k, rwkv_k_a,
              rwkv_r_k, rwkv_ln_w, rwkv_ln_b,
              mla_q_norm, mla_w_uq, mla_kv_norm, mla_w_uk, mla_w_uv,
              w_branch_a, w_branch_b, w_branch_c, w_mix_out,
              norm_ffn2, ffn2_w_in, ffn2_w_out, final_norm):
    n_dec, n_pages = page_table.shape
    past_len = n_pages * PAGE_SIZE
    bp, tp = x_prompt.shape[0], x_prompt.shape[1]
    pos_p = jnp.arange(tp, dtype=jnp.int32)
    pos_s = past_len + jnp.arange(x_sample.shape[1], dtype=jnp.int32)
    xp, xs = x_prompt, x_sample
    new_p, new_s = [], []
    for l in range(DEPTH):
        rwkv_w = (rwkv_mu[l], rwkv_w0[l], rwkv_w_up[l], rwkv_a0[l], rwkv_a_up[l], rwkv_g_up[l],
                  rwkv_k_k[l], rwkv_k_a[l], rwkv_r_k[l], rwkv_ln_w[l], rwkv_ln_b[l])
        mla_w = (mla_q_norm[l], mla_w_uq[l], mla_kv_norm[l], mla_w_uk[l], mla_w_uv[l])
        W = (norm_ffn1[l], ffn1_w_in[l], ffn1_w_out[l], norm_mix[l], w_mix_in[l], rwkv_w, mla_w,
             w_branch_a[l], w_branch_b[l], w_branch_c[l], w_mix_out[l],
             norm_ffn2[l], ffn2_w_in[l], ffn2_w_out[l])
        S0 = jnp.zeros((bp, A_HEADS, A_HEAD_DIM, A_HEAD_DIM), x_prompt.dtype)
        sh0 = jnp.zeros((bp, RWKV_COLS), x_prompt.dtype)
        xp, ent_p = _layer(xp, pos_p, (None, None, None, None), S0, sh0, W)
        past = (cache_mla_latent[l, page_table].reshape(n_dec, past_len, KV_RANK),
                cache_mla_krope[l, page_table].reshape(n_dec, past_len, ROPE_DIM),
                cache_sb_k[l, page_table].reshape(n_dec, past_len, C_HEADS, C_HEAD_DIM),
                cache_sb_v[l, page_table].reshape(n_dec, past_len, C_HEADS, C_HEAD_DIM))
        xs, ent_s = _layer(xs, pos_s, past, state_rwkv[l], state_rwkv_shift[l], W)
        new_p.append(ent_p)
        new_s.append(ent_s)
    y_prompt = rms_norm(xp, final_norm)
    y_sample = rms_norm(xs, final_norm)
    lat_p, kr_p, k_p, v_p, s_p, sh_p = [jnp.stack(t) for t in zip(*new_p)]
    lat_s, kr_s, k_s, v_s, s_s, sh_s = [jnp.stack(t) for t in zip(*new_s)]
    return (y_prompt, y_sample, lat_p, kr_p, k_p, v_p, s_p, sh_p, lat_s, kr_s, k_s, v_s, s_s, sh_s)
```

```python
import functools
import math

import jax
import jax.numpy as jnp
from jax import lax
from jax.experimental import pallas as pl
from jax.experimental.pallas import tpu as pltpu

F32 = jnp.float32
BF16 = jnp.bfloat16

D_MODEL = 1024
PAGE_SIZE = 128
A_HEADS, A_HEAD_DIM = 4, 64
A_WIDTH = A_HEADS * A_HEAD_DIM
W_LORA, A_LORA, G_LORA = 64, 64, 128
RWKV_COLS = 3 * A_WIDTH + W_LORA + A_LORA + G_LORA
GN_EPS = 64e-5
B_HEADS, Q_RANK, KV_RANK, NOPE_DIM, ROPE_DIM, V_DIM = 8, 256, 128, 64, 32, 64
B_WIDTH = B_HEADS * V_DIM
MLA_COLS = Q_RANK + KV_RANK + ROPE_DIM
ROPE_BASE = 10000.0
C_HEADS, C_HEAD_DIM = 4, 64
C_WIDTH = C_HEADS * C_HEAD_DIM
SB_COLS = 3 * C_WIDTH
FFN_DIM = 2816
EPS = 1e-6

LANES = 128
VMEM_LIMIT = 56 * 1024 * 1024

_G0 = 0
_RW0 = 3 * D_MODEL
_MQ0 = _RW0 + RWKV_COLS
_SB0 = _MQ0 + Q_RANK + KV_RANK
_KRA = _SB0 + SB_COLS
_KRB = _KRA + LANES
_WIN_COLS = _KRB + LANES

_NEG = -1e30


def _cparams(*sem):
    return pltpu.CompilerParams(dimension_semantics=sem, vmem_limit_bytes=VMEM_LIMIT)


def _const_spec(shape):
    nd = len(shape)
    return pl.BlockSpec(shape, lambda *_: (0,) * nd, pipeline_mode=pl.Buffered(1))


def _dot(a, b):
    return jnp.dot(a, b, preferred_element_type=F32)


def _dot_nt(a, b):
    return lax.dot_general(a, b, (((1,), (1,)), ((), ())), preferred_element_type=F32)


def _rms(x, g):
    return x * lax.rsqrt(jnp.mean(x * x, axis=-1, keepdims=True) + EPS) * g


def _softplus_parts(z):
    sp = jnp.log1p(jnp.exp(-jnp.abs(z)))
    return jnp.maximum(z, 0.0) + sp, jnp.minimum(z, 0.0) - sp


def _split3(x):
    x1 = x.astype(BF16)
    r1 = x - x1.astype(F32)
    x2 = r1.astype(BF16)
    x3 = (r1 - x2.astype(F32)).astype(BF16)
    return x1, x2, x3


def _exact_dot01(x, m):
    x1, x2, x3 = _split3(x)
    return _dot(x1, m) + _dot(x2, m) + _dot(x3, m)


def _head_ones(width, head):
    r = lax.broadcasted_iota(jnp.int32, (width, width), 0) // head
    c = lax.broadcasted_iota(jnp.int32, (width, width), 1) // head
    return jnp.where(r == c, 1.0, 0.0).astype(BF16)


def _ffn_kernel(*refs, chunks, final):
    if final:
        x_ref, g_ref, win_ref, wout_ref, gf_ref, o_ref = refs
    else:
        x_ref, g_ref, win_ref, wout_ref, o_ref = refs
    x = x_ref[...]
    hb = _rms(x, g_ref[...]).astype(BF16)
    acc = None
    for a, b in chunks:
        gt = _dot(hb, win_ref[:, a:b])
        up = _dot(hb, win_ref[:, FFN_DIM + a:FFN_DIM + b])
        act = (gt * jax.nn.sigmoid(gt) * up).astype(BF16)
        part = _dot(act, wout_ref[a:b, :])
        acc = part if acc is None else acc + part
    y = x + 0.5 * acc
    if final:
        y = _rms(y, gf_ref[...])
    o_ref[...] = y


def _ffn(x, g, win_b, wout_b, final_g=None, tm=512):
    n, d = x.shape
    tm = min(tm, n)
    half = FFN_DIM // 2
    chunks = ((0, half), (half, FFN_DIM))
    in_specs = [pl.BlockSpec((tm, d), lambda i: (i, 0)), _const_spec((1, d)),
                _const_spec(win_b.shape), _const_spec(wout_b.shape)]
    args = [x, g.reshape(1, d), win_b, wout_b]
    if final_g is not None:
        in_specs.append(_const_spec((1, d)))
        args.append(final_g.reshape(1, d))
    return pl.pallas_call(
        functools.partial(_ffn_kernel, chunks=chunks, final=final_g is not None),
        grid=(n // tm,),
        in_specs=in_specs,
        out_specs=pl.BlockSpec((tm, d), lambda i: (i, 0)),
        out_shape=jax.ShapeDtypeStruct((n, d), F32),
        compiler_params=_cparams("parallel"),
    )(*args)


def _proj_common(x_ref, g_ref, win_ref, sg_ref, pa_ref):
    hb = _rms(x_ref[...], g_ref[...]).astype(BF16)
    sg_ref[...] = jax.nn.sigmoid(_dot(hb, win_ref[:, _G0:_RW0]))
    pa_ref[...] = _dot(hb, win_ref[:, _RW0:_MQ0])
    cm = _dot(hb, win_ref[:, _MQ0:_SB0])
    sb = _dot(hb, win_ref[:, _SB0:_KRA])
    kra = _dot(hb, win_ref[:, _KRA:_KRB])
    krb = _dot(hb, win_ref[:, _KRB:_WIN_COLS])
    return cm[:, :Q_RANK], cm[:, Q_RANK:], sb, kra, krb


def _proj_prompt_kernel(x_ref, g_ref, win_ref, tab_ref, qn_ref, kvn_ref, wuq_ref, wuk_ref, wuv_ref,
                        sg_ref, pa_ref, qf_ref, kcat_ref, vv_ref, lat_ref, krope_ref,
                        sbk_ref, sbv_ref, sbqkv_ref):
    cq, ckv, sb, kra, krb = _proj_common(x_ref, g_ref, win_ref, sg_ref, pa_ref)
    sbk_ref[...] = sb[:, C_WIDTH:2 * C_WIDTH]
    sbv_ref[...] = sb[:, 2 * C_WIDTH:]
    sbqkv_ref[...] = sb.astype(BF16)
    tab = tab_ref[...]
    kr = kra * tab[:, 2 * LANES:3 * LANES] + krb * tab[:, 3 * LANES:]
    krope_ref[...] = kr[:, :ROPE_DIM]
    lat = _rms(ckv, kvn_ref[...])
    lat_ref[...] = lat
    latb = lat.astype(BF16)
    hq = _rms(cq, qn_ref[...]).astype(BF16)
    q1 = _dot(hq, wuq_ref[0])
    q2 = _dot(hq, wuq_ref[1])
    kn = _dot(latb, wuk_ref[...])
    kadd = pltpu.roll(kr, NOPE_DIM, 1)
    cq_t, sq_t = tab[:, :LANES], tab[:, LANES:2 * LANES]
    for h in range(B_HEADS):
        sl = slice(h * LANES, (h + 1) * LANES)
        qf_ref[:, sl] = (q1[:, sl] * cq_t + q2[:, sl] * sq_t).astype(BF16)
        kcat_ref[:, sl] = (kn[:, sl] + kadd).astype(BF16)
    vv_ref[...] = _dot(latb, wuv_ref[...]).astype(BF16)


def _proj_sample_kernel(x_ref, g_ref, win_ref, tab_ref, qn_ref, kvn_ref, wuq_ref, wukt_ref,
                        sg_ref, pa_ref, qlat_ref, qrope_ref, lat_ref, krope_ref,
                        sbk_ref, sbv_ref, sbq_ref):
    cq, ckv, sb, kra, krb = _proj_common(x_ref, g_ref, win_ref, sg_ref, pa_ref)
    sbq_ref[...] = sb[:, :C_WIDTH].astype(BF16)
    sbk_ref[...] = sb[:, C_WIDTH:2 * C_WIDTH]
    sbv_ref[...] = sb[:, 2 * C_WIDTH:]
    tab = tab_ref[...]
    rw = B_HEADS * ROPE_DIM
    kr = kra * tab[:, 2 * rw:2 * rw + LANES] + krb * tab[:, 2 * rw + LANES:]
    krope_ref[...] = kr[:, :ROPE_DIM]
    lat_ref[...] = _rms(ckv, kvn_ref[...])
    hq = _rms(cq, qn_ref[...]).astype(BF16)
    qs = _dot(hq, wuq_ref[...])
    nw = B_HEADS * NOPE_DIM
    qn = qs[:, :nw].astype(BF16)
    qrope_ref[...] = (qs[:, nw:nw + rw] * tab[:, :rw] + qs[:, nw + rw:] * tab[:, rw:2 * rw]).astype(BF16)
    for h in range(B_HEADS):
        p = h // 2
        qlat_ref[:, h * LANES:(h + 1) * LANES] = _dot(
            qn[:, p * LANES:(p + 1) * LANES], wukt_ref[h]).astype(BF16)


def _proj_prompt(x, g, w, tab, seq, tm=256):
    n, d = x.shape
    nt = seq // tm
    row = lambda c: pl.BlockSpec((tm, c), lambda i: (i, 0))
    outs = [(3 * D_MODEL, F32), (RWKV_COLS, F32), (B_HEADS * LANES, BF16), (B_HEADS * LANES, BF16),
            (B_WIDTH, BF16), (KV_RANK, F32), (ROPE_DIM, F32), (C_WIDTH, F32), (C_WIDTH, F32),
            (SB_COLS, BF16)]
    return pl.pallas_call(
        _proj_prompt_kernel,
        grid=(n // tm,),
        in_specs=[row(d), _const_spec((1, d)), _const_spec(w["win"].shape),
                  pl.BlockSpec((tm, 4 * LANES), lambda i: (i % nt, 0)),
                  _const_spec((1, Q_RANK)), _const_spec((1, KV_RANK)),
                  _const_spec(w["wuq_p"].shape), _const_spec(w["wuk_p"].shape),
                  _const_spec(w["wuv"].shape)],
        out_specs=[row(c) for c, _ in outs],
        out_shape=[jax.ShapeDtypeStruct((n, c), dt) for c, dt in outs],
        compiler_params=_cparams("parallel"),
    )(x, g.reshape(1, d), w["win"], tab, w["qn"], w["kvn"], w["wuq_p"], w["wuk_p"], w["wuv"])


def _proj_sample(x, g, w, tab):
    n, d = x.shape
    tm = n
    row = lambda c: pl.BlockSpec((tm, c), lambda i: (i, 0))
    outs = [(3 * D_MODEL, F32), (RWKV_COLS, F32), (B_HEADS * LANES, BF16), (B_HEADS * ROPE_DIM, BF16),
            (KV_RANK, F32), (ROPE_DIM, F32), (C_WIDTH, F32), (C_WIDTH, F32), (C_WIDTH, BF16)]
    return pl.pallas_call(
        _proj_sample_kernel,
        grid=(n // tm,),
        in_specs=[row(d), _const_spec((1, d)), _const_spec(w["win"].shape), row(tab.shape[1]),
                  _const_spec((1, Q_RANK)), _const_spec((1, KV_RANK)),
                  _const_spec(w["wuq_s"].shape), _const_spec(w["wukt"].shape)],
        out_specs=[row(c) for c, _ in outs],
        out_shape=[jax.ShapeDtypeStruct((n, c), dt) for c, dt in outs],
        compiler_params=_cparams("parallel"),
    )(x, g.reshape(1, d), w["win"], tab, w["qn"], w["kvn"], w["wuq_s"], w["wukt"])


def _rwkv_prep_kernel(*refs, prompt):
    if prompt:
        (pa_ref, prev_ref, mu_ref, vec_ref, wup_ref, aup_ref, gup_ref,
         sin_ref, pin_ref, vt_ref) = refs
        p = pa_ref[...]
        tt = p.shape[0]
        first = pl.program_id(1) == 0
        prev_row = jnp.where(first, 0.0, prev_ref[7:8, :])
        rows = lax.broadcasted_iota(jnp.int32, p.shape, 0)
        p_prev = jnp.where(rows == 0, prev_row, pltpu.roll(p, 1, 0))
    else:
        (pa_ref, prev_ref, mu_ref, vec_ref, wup_ref, aup_ref, gup_ref,
         sin_ref, pin_ref) = refs
        p = pa_ref[...]
        p_prev = prev_ref[...]
    xm = p + (p_prev - p) * mu_ref[...]
    aw = A_WIDTH
    r, k, v = xm[:, :aw], xm[:, aw:2 * aw], xm[:, 2 * aw:3 * aw]
    wa = xm[:, 3 * aw:3 * aw + LANES]
    gd = xm[:, 3 * aw + LANES:]
    vec = vec_ref[...]
    w0, a0, k_k, k_a, r_k = (vec[i:i + 1, :] for i in range(5))
    lw = _dot(jnp.tanh(wa).astype(BF16), wup_ref[...])
    la = _dot(wa.astype(BF16), aup_ref[...])
    sp, _ = _softplus_parts(-(w0 + lw))
    decay = jnp.exp(-jnp.exp(-sp - 0.5))
    a = jax.nn.sigmoid(a0 + la)
    g = _dot(jax.nn.sigmoid(gd).astype(BF16), gup_ref[...])
    ones = _head_ones(aw, A_HEAD_DIM)
    kk = k * k_k
    kk = kk / jnp.maximum(jnp.sqrt(_exact_dot01(kk * kk, ones)), 1e-12)
    km = k * (1.0 + (a - 1.0) * k_a)
    b = kk * a
    sin_ref[:, 0 * aw:1 * aw] = decay
    sin_ref[:, 1 * aw:2 * aw] = -kk
    sin_ref[:, 2 * aw:3 * aw] = b
    sin_ref[:, 3 * aw:4 * aw] = km
    sin_ref[:, 4 * aw:5 * aw] = decay * r
    pin_ref[:, 0 * aw:1 * aw] = v
    pin_ref[:, 1 * aw:2 * aw] = _exact_dot01(b * r, ones)
    pin_ref[:, 2 * aw:3 * aw] = _exact_dot01(km * r, ones)
    pin_ref[:, 3 * aw:4 * aw] = _exact_dot01(r * km * r_k, ones)
    pin_ref[:, 4 * aw:5 * aw] = g
    if prompt:
        vt = v.T
        for h in range(A_HEADS):
            vt_ref[h] = vt[h * A_HEAD_DIM:(h + 1) * A_HEAD_DIM, :]


def _rwkv_prep_prompt(pa, w, batch, seq, tt=256):
    n = pa.shape[0]
    nt = seq // tt
    pa3 = pa.reshape(batch, seq, RWKV_COLS)
    aw5 = 5 * A_WIDTH
    sin, pin, vt = pl.pallas_call(
        functools.partial(_rwkv_prep_kernel, prompt=True),
        grid=(batch, nt),
        in_specs=[pl.BlockSpec((None, tt, RWKV_COLS), lambda b, i: (b, i, 0)),
                  pl.BlockSpec((None, 8, RWKV_COLS), lambda b, i: (b, jnp.maximum(i * (tt // 8) - 1, 0), 0)),
                  _const_spec((1, RWKV_COLS)), _const_spec((8, A_WIDTH)),
                  _const_spec((LANES, A_WIDTH)), _const_spec((LANES, A_WIDTH)),
                  _const_spec((G_LORA, A_WIDTH))],
        out_specs=[pl.BlockSpec((None, tt, aw5), lambda b, i: (b, i, 0)),
                   pl.BlockSpec((None, tt, aw5), lambda b, i: (b, i, 0)),
                   pl.BlockSpec((None, A_HEADS, A_HEAD_DIM, tt), lambda b, i: (b, 0, 0, i))],
        out_shape=[jax.ShapeDtypeStruct((batch, seq, aw5), F32),
                   jax.ShapeDtypeStruct((batch, seq, aw5), F32),
                   jax.ShapeDtypeStruct((batch, A_HEADS, A_HEAD_DIM, seq), F32)],
        compiler_params=_cparams("parallel", "parallel"),
    )(pa3, pa3, w["mu"], w["rvec"], w["wup"], w["aup"], w["gup"])
    return sin, pin, vt


def _rwkv_prep_sample(pa, shift, w):
    n = pa.shape[0]
    aw5 = 5 * A_WIDTH
    full = lambda c: pl.BlockSpec((n, c), lambda i: (0, 0))
    sin, pin = pl.pallas_call(
        functools.partial(_rwkv_prep_kernel, prompt=False),
        grid=(1,),
        in_specs=[full(RWKV_COLS), full(RWKV_COLS),
                  _const_spec((1, RWKV_COLS)), _const_spec((8, A_WIDTH)),
                  _const_spec((LANES, A_WIDTH)), _const_spec((LANES, A_WIDTH)),
                  _const_spec((G_LORA, A_WIDTH))],
        out_specs=[full(aw5), full(aw5)],
        out_shape=[jax.ShapeDtypeStruct((n, aw5), F32), jax.ShapeDtypeStruct((n, aw5), F32)],
        compiler_params=_cparams("arbitrary"),
    )(pa, shift, w["mu"], w["rvec"], w["wup"], w["aup"], w["gup"])
    return sin, pin


def _rwkv_scan_kernel(sin_ref, vt_ref, s0_ref, sat_ref, ypt_ref, sfin_ref, s_ref, *, bb, tt):
    tb = pl.program_id(1)

    @pl.when(tb == 0)
    def _():
        s_ref[...] = s0_ref[...]

    sat_ref[...] = jnp.zeros_like(sat_ref)
    ypt_ref[...] = jnp.zeros_like(ypt_ref)
    aw = A_WIDTH
    lane_t = lax.broadcasted_iota(jnp.int32, (A_HEAD_DIM, tt), 1)
    lane_j = lax.broadcasted_iota(jnp.int32, (1, LANES), 1)

    sub = min(tt, 8)

    def body(tg, carry):
        t0 = 0 if tt == sub else pl.multiple_of(tg * sub, sub)
        for b in range(bb):
            for h in range(A_HEADS):
                p = h // 2
                own = (lane_j // A_HEAD_DIM) == (h % 2)
                tile = lambda q: sin_ref[b, pl.ds(t0, sub), q * aw + p * LANES:q * aw + (p + 1) * LANES]
                w8, kkn8, wr8 = tile(0), tile(1), tile(4)
                bm8 = jnp.where(own, tile(2), 0.0)
                kmm8 = jnp.where(own, tile(3), 0.0)
                s = s_ref[b, h]
                vt = vt_ref[b, h]
                sa_buf, yp_buf = sat_ref[b, h], ypt_ref[b, h]
                for u in range(sub):
                    row = lambda x: x[u:u + 1, :]
                    sel = lane_t == t0 + u
                    sa = jnp.sum(s * row(kkn8), axis=1, keepdims=True)
                    yp = jnp.sum(s * row(wr8), axis=1, keepdims=True)
                    vc = jnp.sum(jnp.where(sel, vt, 0.0), axis=1, keepdims=True)
                    s = s * row(w8) + sa * row(bm8) + vc * row(kmm8)
                    sa_buf = jnp.where(sel, sa, sa_buf)
                    yp_buf = jnp.where(sel, yp, yp_buf)
                s_ref[b, h] = s
                sat_ref[b, h] = sa_buf
                ypt_ref[b, h] = yp_buf
        return carry

    if tt == sub:
        body(0, 0)
    else:
        lax.fori_loop(0, tt // sub, body, 0)

    @pl.when(tb == pl.num_programs(1) - 1)
    def _():
        sfin_ref[...] = s_ref[...]


def _rwkv_scan(sin, vt, s0p, bb, tt):
    batch, seq, aw5 = sin.shape
    hblk = lambda last: pl.BlockSpec((bb, A_HEADS, A_HEAD_DIM, last), lambda g, i: (g, 0, 0, i))
    sblk = pl.BlockSpec((bb, A_HEADS, A_HEAD_DIM, LANES), lambda g, i: (g, 0, 0, 0))
    return pl.pallas_call(
        functools.partial(_rwkv_scan_kernel, bb=bb, tt=tt),
        grid=(batch // bb, seq // tt),
        in_specs=[pl.BlockSpec((bb, tt, aw5), lambda g, i: (g, i, 0)), hblk(tt), sblk],
        out_specs=[hblk(tt), hblk(tt), sblk],
        out_shape=[jax.ShapeDtypeStruct((batch, A_HEADS, A_HEAD_DIM, seq), F32),
                   jax.ShapeDtypeStruct((batch, A_HEADS, A_HEAD_DIM, seq), F32),
                   jax.ShapeDtypeStruct((batch, A_HEADS, A_HEAD_DIM, LANES), F32)],
        scratch_shapes=[pltpu.VMEM((bb, A_HEADS, A_HEAD_DIM, LANES), F32)],
        compiler_params=_cparams("parallel", "arbitrary"),
    )(sin, vt, s0p)


def _rwkv_post_kernel(sa_ref, yp_ref, pin_ref, ln_ref, ya_ref, *, transposed):
    aw = A_WIDTH
    if transposed:
        tm = sa_ref.shape[-1]
        sa = sa_ref[...].reshape(aw, tm).T
        yp = yp_ref[...].reshape(aw, tm).T
    else:
        sa, yp = sa_ref[...], yp_ref[...]
    pin = pin_ref[...]
    v, brb, krb, bc, g = (pin[:, i * aw:(i + 1) * aw] for i in range(5))
    y = yp + sa * brb + v * krb
    ones = _head_ones(aw, A_HEAD_DIM)
    inv = 1.0 / A_HEAD_DIM
    mean = _exact_dot01(y, ones) * inv
    dlt = y - mean
    var = _exact_dot01(dlt * dlt, ones) * inv
    yn = dlt * lax.rsqrt(var + GN_EPS) * ln_ref[0:1, :] + ln_ref[1:2, :]
    ya_ref[...] = ((yn + bc * v) * g).astype(BF16)


def _rwkv_post_prompt(sat, ypt, pin, ln, tm=256):
    batch, seq, aw5 = pin.shape
    hblk = pl.BlockSpec((None, A_HEADS, A_HEAD_DIM, tm), lambda b, i: (b, 0, 0, i))
    return pl.pallas_call(
        functools.partial(_rwkv_post_kernel, transposed=True),
        grid=(batch, seq // tm),
        in_specs=[hblk, hblk, pl.BlockSpec((None, tm, aw5), lambda b, i: (b, i, 0)),
                  _const_spec((8, A_WIDTH))],
        out_specs=pl.BlockSpec((None, tm, A_WIDTH), lambda b, i: (b, i, 0)),
        out_shape=jax.ShapeDtypeStruct((batch, seq, A_WIDTH), BF16),
        compiler_params=_cparams("parallel", "parallel"),
    )(sat, ypt, pin, ln)


def _rwkv_post_sample(sa, yp, pin, ln):
    n = pin.shape[0]
    full = lambda c: pl.BlockSpec((n, c), lambda i: (0, 0))
    return pl.pallas_call(
        functools.partial(_rwkv_post_kernel, transposed=False),
        grid=(1,),
        in_specs=[full(A_WIDTH), full(A_WIDTH), full(5 * A_WIDTH), _const_spec((8, A_WIDTH))],
        out_specs=full(A_WIDTH),
        out_shape=jax.ShapeDtypeStruct((n, A_WIDTH), BF16),
        compiler_params=_cparams("arbitrary"),
    )(sa, yp, pin, ln)


def _mla_prompt_kernel(q_ref, k_ref, v_ref, o_ref, *, tq, scale):
    i = pl.program_id(1)
    row = lax.broadcasted_iota(jnp.int32, (tq, tq), 0)
    col = lax.broadcasted_iota(jnp.int32, (tq, tq), 1)
    causal = col <= row
    lane = lax.broadcasted_iota(jnp.int32, (tq, LANES), 1)
    for p in range(B_HEADS // 2):
        psl = slice(p * LANES, (p + 1) * LANES)
        res = []
        for hh in range(2):
            h = 2 * p + hh
            hsl = slice(h * LANES, (h + 1) * LANES)
            q = q_ref[:, hsl]

            def step(j, carry, diag, q=q, hsl=hsl, psl=psl):
                m, l, acc = carry
                off = pl.multiple_of(j * tq, tq)
                s = _dot_nt(q, k_ref[pl.ds(off, tq), hsl]) * scale
                if diag:
                    s = jnp.where(causal, s, _NEG)
                m_new = jnp.maximum(m, jnp.max(s, axis=-1, keepdims=True))
                alpha = jnp.exp(m - m_new)
                pr = jnp.exp(s - m_new)
                l = alpha * l + jnp.sum(pr, axis=-1, keepdims=True)
                acc = alpha * acc + _dot(pr.astype(BF16), v_ref[pl.ds(off, tq), psl])
                return m_new, l, acc

            init = (jnp.full((tq, 1), _NEG, F32), jnp.zeros((tq, 1), F32), jnp.zeros((tq, LANES), F32))
            carry = lax.fori_loop(0, i, functools.partial(step, diag=False), init)
            m, l, acc = step(i, carry, True)
            res.append(acc / l)
        o_ref[:, psl] = jnp.where(lane < V_DIM, res[0], res[1]).astype(BF16)


def _mla_prompt(qf, kcat, vv, batch, seq, tq=256):
    hw = B_HEADS * LANES
    scale = 1.0 / math.sqrt(NOPE_DIM + ROPE_DIM)
    return pl.pallas_call(
        functools.partial(_mla_prompt_kernel, tq=tq, scale=scale),
        grid=(batch, seq // tq),
        in_specs=[pl.BlockSpec((None, tq, hw), lambda b, i: (b, i, 0)),
                  pl.BlockSpec((None, seq, hw), lambda b, i: (b, 0, 0)),
                  pl.BlockSpec((None, seq, B_WIDTH), lambda b, i: (b, 0, 0))],
        out_specs=pl.BlockSpec((None, tq, B_WIDTH), lambda b, i: (b, i, 0)),
        out_shape=jax.ShapeDtypeStruct((batch, seq, B_WIDTH), BF16),
        compiler_params=_cparams("parallel", "arbitrary"),
    )(qf.reshape(batch, seq, hw), kcat.reshape(batch, seq, hw), vv.reshape(batch, seq, B_WIDTH))


def _sb_weights(z, c, upper, valid):
    soft, logsig = _softplus_parts(z)
    if valid is not None:
        soft = jnp.where(valid, soft, 0.0)
    hi = soft.astype(BF16)
    lo = (soft - hi.astype(F32)).astype(BF16)
    later = _dot(hi, upper) + _dot(lo, upper)
    a = jnp.exp(logsig - later - c)
    if valid is not None:
        a = jnp.where(valid, a, 0.0)
    return a, c + jnp.sum(soft, axis=-1, keepdims=True)


def _upper_ones(tk):
    r = lax.broadcasted_iota(jnp.int32, (tk, tk), 0)
    c = lax.broadcasted_iota(jnp.int32, (tk, tk), 1)
    return jnp.where(r > c, 1.0, 0.0).astype(BF16)


def _sb_prompt_kernel(q_ref, k_ref, v_ref, o_ref, *, tq, scale):
    i = pl.program_id(1)
    row = lax.broadcasted_iota(jnp.int32, (tq, tq), 0)
    col = lax.broadcasted_iota(jnp.int32, (tq, tq), 1)
    strict = col < row
    upper = _upper_ones(tq)
    lane = lax.broadcasted_iota(jnp.int32, (tq, LANES), 1)
    for p in range(C_HEADS // 2):
        psl = slice(p * LANES, (p + 1) * LANES)
        qp = q_ref[:, psl].astype(F32)
        res = []
        for hh in range(2):
            qm = jnp.where((lane // C_HEAD_DIM) == hh, qp, 0.0).astype(BF16)

            def step(j, carry, diag, qm=qm, psl=psl):
                c, acc = carry
                off = pl.multiple_of(j * tq, tq)
                z = _dot_nt(qm, k_ref[pl.ds(off, tq), psl]) * scale
                a, c = _sb_weights(z, c, upper, strict if diag else None)
                acc = acc + _dot(a.astype(BF16), v_ref[pl.ds(off, tq), psl])
                return c, acc

            carry = step(i, (jnp.zeros((tq, 1), F32), jnp.zeros((tq, LANES), F32)), True)
            _, acc = lax.fori_loop(0, i, lambda jj, cr: step(i - 1 - jj, cr, False), carry)
            res.append(acc)
        o_ref[:, psl] = jnp.where(lane < C_HEAD_DIM, res[0], res[1]).astype(BF16)


def _sb_prompt(sbqkv, batch, seq, tq=256):
    x = sbqkv.reshape(batch, seq, SB_COLS)
    scale = C_HEAD_DIM ** -0.5
    return pl.pallas_call(
        functools.partial(_sb_prompt_kernel, tq=tq, scale=scale),
        grid=(batch, seq // tq),
        in_specs=[pl.BlockSpec((None, tq, C_WIDTH), lambda b, i: (b, i, 0)),
                  pl.BlockSpec((None, seq, C_WIDTH), lambda b, i: (b, 0, 1)),
                  pl.BlockSpec((None, seq, C_WIDTH), lambda b, i: (b, 0, 2))],
        out_specs=pl.BlockSpec((None, tq, C_WIDTH), lambda b, i: (b, i, 0)),
        out_shape=jax.ShapeDtypeStruct((batch, seq, C_WIDTH), BF16),
        compiler_params=_cparams("parallel", "arbitrary"),
    )(x, x, x)


def _decode_kernel(pt_ref, qlat_ref, qrope_ref, sbq_ref, latn_ref, krn_ref, *rest, npg, scale_b, scale_c):
    pages = rest[:4 * npg]
    olat_ref, osb_ref, m_ref, l_ref, acc_ref, c_ref, accs_ref, qm_ref = rest[4 * npg:]
    s_idx = pl.program_id(1)
    rows = 8
    lane_c = lax.broadcasted_iota(jnp.int32, (rows, C_WIDTH), 1)
    row_c = lax.broadcasted_iota(jnp.int32, (rows, C_WIDTH), 0)
    own = (lane_c // C_HEAD_DIM) == row_c

    @pl.when(s_idx == 0)
    def _():
        ql = qlat_ref[...].astype(F32)
        qr = qrope_ref[...].astype(F32)
        latn = latn_ref[...].astype(BF16).astype(F32)
        krn = krn_ref[...].astype(BF16).astype(F32)
        s_self = (jnp.sum(ql * latn, axis=-1, keepdims=True)
                  + jnp.sum(qr * krn, axis=-1, keepdims=True)) * scale_b
        m_ref[...] = jnp.broadcast_to(s_self, m_ref.shape)
        l_ref[...] = jnp.ones_like(l_ref)
        acc_ref[...] = jnp.broadcast_to(latn, acc_ref.shape)
        c_ref[...] = jnp.zeros_like(c_ref)
        accs_ref[...] = jnp.zeros_like(accs_ref)
        qm_ref[...] = jnp.where(own, jnp.broadcast_to(sbq_ref[...].astype(F32), qm_ref.shape), 0.0)

    upper = _upper_ones(PAGE_SIZE)
    ql, qr, qm = qlat_ref[...], qrope_ref[...], qm_ref[...].astype(BF16)
    m, l, acc = m_ref[:, 0:1], l_ref[:, 0:1], acc_ref[...]
    c, accs = c_ref[:, 0:1], accs_ref[...]
    for g in range(npg):
        lat_ref, kr_ref, k_ref, v_ref = pages[4 * g:4 * g + 4]
        latp = lat_ref[...].astype(BF16)
        s = (_dot_nt(ql, latp) + _dot_nt(qr, kr_ref[...].astype(BF16))) * scale_b
        m_new = jnp.maximum(m, jnp.max(s, axis=-1, keepdims=True))
        alpha = jnp.exp(m - m_new)
        pr = jnp.exp(s - m_new)
        l = alpha * l + jnp.sum(pr, axis=-1, keepdims=True)
        acc = alpha * acc + _dot(pr.astype(BF16), latp)
        m = m_new
        z = _dot_nt(qm, k_ref[...].astype(BF16)) * scale_c
        a, c = _sb_weights(z, c, upper, None)
        accs = accs + _dot(a.astype(BF16), v_ref[...].astype(BF16))
    m_ref[...] = jnp.broadcast_to(m, m_ref.shape)
    l_ref[...] = jnp.broadcast_to(l, l_ref.shape)
    acc_ref[...] = acc
    c_ref[...] = jnp.broadcast_to(c, c_ref.shape)
    accs_ref[...] = accs

    @pl.when(s_idx == pl.num_programs(1) - 1)
    def _():
        olat_ref[...] = acc / l
        osb_ref[...] = jnp.sum(jnp.where(own, accs, 0.0), axis=0, keepdims=True)


def _decode_attn(layer, page_table, qlat, qrope, sbq, lat_new, kr_new, c_lat, c_kr, c_k, c_v, npg=8):
    n, n_pages = page_table.shape
    c_k = c_k.reshape(c_k.shape[:3] + (C_WIDTH,))
    c_v = c_v.reshape(c_v.shape[:3] + (C_WIDTH,))
    steps = n_pages // npg

    def page_spec(arr, g):
        def imap(b, s, pt):
            return (layer, pt[b, n_pages - 1 - (s * npg + g)], 0, 0)
        return pl.BlockSpec((None, None, PAGE_SIZE, arr.shape[-1]), imap)

    tok = lambda r, c: pl.BlockSpec((None, r, c), lambda b, s, pt: (b, 0, 0))
    in_specs = [tok(B_HEADS, KV_RANK), tok(B_HEADS, ROPE_DIM), tok(1, C_WIDTH), tok(1, KV_RANK),
                tok(1, ROPE_DIM)]
    args = [qlat.reshape(n, B_HEADS, KV_RANK), qrope.reshape(n, B_HEADS, ROPE_DIM),
            sbq.reshape(n, 1, C_WIDTH), lat_new.reshape(n, 1, KV_RANK), kr_new.reshape(n, 1, ROPE_DIM)]
    for g in range(npg):
        for arr in (c_lat, c_kr, c_k, c_v):
            in_specs.append(page_spec(arr, g))
            args.append(arr)
    grid_spec = pltpu.PrefetchScalarGridSpec(
        num_scalar_prefetch=1,
        grid=(n, steps),
        in_specs=in_specs,
        out_specs=[tok(B_HEADS, KV_RANK), tok(1, C_WIDTH)],
        scratch_shapes=[pltpu.VMEM((8, LANES), F32), pltpu.VMEM((8, LANES), F32),
                        pltpu.VMEM((8, KV_RANK), F32), pltpu.VMEM((8, LANES), F32),
                        pltpu.VMEM((8, C_WIDTH), F32), pltpu.VMEM((8, C_WIDTH), F32)],
    )
    olat, osb = pl.pallas_call(
        functools.partial(_decode_kernel, npg=npg, scale_b=1.0 / math.sqrt(NOPE_DIM + ROPE_DIM),
                          scale_c=C_HEAD_DIM ** -0.5),
        grid_spec=grid_spec,
        out_shape=[jax.ShapeDtypeStruct((n, B_HEADS, KV_RANK), F32),
                   jax.ShapeDtypeStruct((n, 1, C_WIDTH), F32)],
        compiler_params=_cparams("parallel", "arbitrary"),
    )(page_table, *args)
    return olat.reshape(n, B_HEADS * KV_RANK), osb.reshape(n, C_WIDTH)


def _merge_kernel(*refs, absorbed):
    if absorbed:
        x_ref, sg_ref, ya_ref, yb_ref, yc_ref, wuv_ref, wba_ref, wbb_ref, wbc_ref, wo_ref, o_ref = refs
        yb = _dot(yb_ref[...].astype(BF16), wuv_ref[...]).astype(BF16)
    else:
        x_ref, sg_ref, ya_ref, yb_ref, yc_ref, wba_ref, wbb_ref, wbc_ref, wo_ref, o_ref = refs
        yb = yb_ref[...]
    d = D_MODEL
    merged = (sg_ref[:, :d] * _dot(ya_ref[...], wba_ref[...])
              + sg_ref[:, d:2 * d] * _dot(yb, wbb_ref[...])
              + sg_ref[:, 2 * d:] * _dot(yc_ref[...].astype(BF16), wbc_ref[...]))
    o_ref[...] = x_ref[...] + _dot(merged.astype(BF16), wo_ref[...])


def _merge(x, sg, ya, yb, yc, w, absorbed, tm=512):
    n, d = x.shape
    tm = min(tm, n)
    row = lambda c: pl.BlockSpec((tm, c), lambda i: (i, 0))
    in_specs = [row(d), row(3 * d), row(A_WIDTH), row(yb.shape[1]), row(C_WIDTH)]
    args = [x, sg, ya, yb, yc]
    if absorbed:
        in_specs.append(_const_spec(w["wuv_bd"].shape))
        args.append(w["wuv_bd"])
    for name in ("wba", "wbb", "wbc", "wo"):
        in_specs.append(_const_spec(w[name].shape))
        args.append(w[name])
    return pl.pallas_call(
        functools.partial(_merge_kernel, absorbed=absorbed),
        grid=(n // tm,),
        in_specs=in_specs,
        out_specs=row(d),
        out_shape=jax.ShapeDtypeStruct((n, d), F32),
        compiler_params=_cparams("parallel"),
    )(*args)


def _rot_cols(w):
    half = w.shape[-1] // 2
    return jnp.concatenate([-w[..., half:], w[..., :half]], axis=-1)


def _pad_cols(w, width):
    return jnp.pad(w, [(0, 0)] * (w.ndim - 1) + [(0, width - w.shape[-1])])


def _prep_weights(l, P):
    w = {}
    win = P["w_mix_in"][l]
    o = _RW0 + RWKV_COLS
    kr = win[:, o + Q_RANK + KV_RANK:o + MLA_COLS]
    w["win"] = jnp.concatenate(
        [win[:, :o + Q_RANK + KV_RANK], win[:, o + MLA_COLS:], _pad_cols(kr, LANES),
         _pad_cols(_rot_cols(kr), LANES)], axis=1).astype(BF16)
    w["qn"] = P["mla_q_norm"][l].reshape(1, Q_RANK)
    w["kvn"] = P["mla_kv_norm"][l].reshape(1, KV_RANK)
    wuq = P["mla_w_uq"][l].reshape(Q_RANK, B_HEADS, NOPE_DIM + ROPE_DIM)
    nope, rope = wuq[..., :NOPE_DIM], wuq[..., NOPE_DIM:]
    z_n, z_r = jnp.zeros_like(nope), jnp.zeros_like(rope)
    w["wuq_p"] = jnp.stack([
        jnp.concatenate([nope, rope, z_r], -1).reshape(Q_RANK, B_HEADS * LANES),
        jnp.concatenate([z_n, _rot_cols(rope), z_r], -1).reshape(Q_RANK, B_HEADS * LANES)]).astype(BF16)
    w["wuq_s"] = jnp.concatenate(
        [nope.reshape(Q_RANK, -1), rope.reshape(Q_RANK, -1), _rot_cols(rope).reshape(Q_RANK, -1)],
        axis=1).astype(BF16)
    wuk = P["mla_w_uk"][l].reshape(KV_RANK, B_HEADS, NOPE_DIM)
    w["wuk_p"] = jnp.concatenate([wuk, jnp.zeros_like(wuk)], -1).reshape(KV_RANK, B_HEADS * LANES).astype(BF16)
    wukt = jnp.transpose(wuk, (1, 2, 0))
    zt = jnp.zeros_like(wukt)
    even = jnp.concatenate([wukt, zt], axis=1)
    odd = jnp.concatenate([zt, wukt], axis=1)
    w["wukt"] = jnp.where((jnp.arange(B_HEADS) % 2 == 0)[:, None, None], even, odd).astype(BF16)
    w["wuv"] = P["mla_w_uv"][l].astype(BF16)
    wuv_h = P["mla_w_uv"][l].reshape(KV_RANK, B_HEADS, V_DIM)
    eye = jnp.eye(B_HEADS, dtype=F32)
    w["wuv_bd"] = jnp.einsum("chd,hg->hcgd", wuv_h, eye).reshape(B_HEADS * KV_RANK, B_WIDTH).astype(BF16)
    w["mu"] = P["rwkv_mu"][l].reshape(1, RWKV_COLS)
    rvec = jnp.stack([P["rwkv_w0"][l], P["rwkv_a0"][l], P["rwkv_k_k"][l], P["rwkv_k_a"][l],
                      P["rwkv_r_k"][l].reshape(A_WIDTH)])
    w["rvec"] = jnp.pad(rvec, ((0, 3), (0, 0)))
    zl = jnp.zeros((W_LORA, A_WIDTH), F32)
    w["wup"] = jnp.concatenate([P["rwkv_w_up"][l], zl], 0).astype(BF16)
    w["aup"] = jnp.concatenate([zl, P["rwkv_a_up"][l]], 0).astype(BF16)
    w["gup"] = P["rwkv_g_up"][l].astype(BF16)
    w["ln"] = jnp.pad(jnp.stack([P["rwkv_ln_w"][l], P["rwkv_ln_b"][l]]), ((0, 6), (0, 0)))
    for name, key in (("wba", "w_branch_a"), ("wbb", "w_branch_b"), ("wbc", "w_branch_c"),
                      ("wo", "w_mix_out"), ("f1i", "ffn1_w_in"), ("f1o", "ffn1_w_out"),
                      ("f2i", "ffn2_w_in"), ("f2o", "ffn2_w_out")):
        w[name] = P[key][l].astype(BF16)
    return w


def _rope_tables(pos):
    half = ROPE_DIM // 2
    inv = ROPE_BASE ** (-jnp.arange(half, dtype=F32) / half)
    ang = pos.astype(F32)[:, None] * inv[None, :]
    c, s = jnp.cos(ang), jnp.sin(ang)
    return jnp.concatenate([c, c], 1), jnp.concatenate([s, s], 1)


def _tab_prompt(seq):
    c, s = _rope_tables(jnp.arange(seq, dtype=jnp.int32))
    one = jnp.ones((seq, NOPE_DIM), F32)
    zn = jnp.zeros((seq, NOPE_DIM), F32)
    zr = jnp.zeros((seq, LANES - NOPE_DIM - ROPE_DIM), F32)
    return jnp.concatenate([one, c, zr, zn, s, zr, _pad_cols(c, LANES), _pad_cols(s, LANES)], axis=1)


def _tab_sample(n, dec_seq, past_len):
    pos = past_len + (jnp.arange(n, dtype=jnp.int32) % dec_seq)
    c, s = _rope_tables(pos)
    return jnp.concatenate([jnp.tile(c, (1, B_HEADS)), jnp.tile(s, (1, B_HEADS)),
                            _pad_cols(c, LANES), _pad_cols(s, LANES)], axis=1)


def _pad_state(s0):
    z = jnp.zeros_like(s0)
    even = jnp.concatenate([s0, z], -1)
    odd = jnp.concatenate([z, s0], -1)
    return jnp.where((jnp.arange(A_HEADS) % 2 == 0)[None, :, None, None], even, odd)


def _unpad_state(sp):
    even, odd = sp[..., :A_HEAD_DIM], sp[..., A_HEAD_DIM:]
    return jnp.where((jnp.arange(A_HEADS) % 2 == 0)[None, :, None, None], even, odd)


def _layer_prompt(x, w, tab, batch, seq, final_g):
    x = _ffn(x, w["n1"], w["f1i"], w["f1o"])
    sg, pa, qf, kcat, vv, lat, krope, sbk, sbv, sbqkv = _proj_prompt(x, w["nm"], w, tab, seq)
    sin, pin, vt = _rwkv_prep_prompt(pa, w, batch, seq)
    s0p = jnp.zeros((batch, A_HEADS, A_HEAD_DIM, LANES), F32)
    sat, ypt, sfin = _rwkv_scan(sin, vt, s0p, bb=4, tt=128)
    ya = _rwkv_post_prompt(sat, ypt, pin, w["ln"]).reshape(batch * seq, A_WIDTH)
    yb = _mla_prompt(qf, kcat, vv, batch, seq).reshape(batch * seq, B_WIDTH)
    yc = _sb_prompt(sbqkv, batch, seq).reshape(batch * seq, C_WIDTH)
    x = _merge(x, sg, ya, yb, yc, w, absorbed=False)
    x = _ffn(x, w["n2"], w["f2i"], w["f2o"], final_g=final_g)
    shift = pa.reshape(batch, seq, RWKV_COLS)[:, -1]
    ent = (lat.reshape(batch, seq, KV_RANK), krope.reshape(batch, seq, ROPE_DIM),
           sbk.reshape(batch, seq, C_HEADS, C_HEAD_DIM), sbv.reshape(batch, seq, C_HEADS, C_HEAD_DIM),
           _unpad_state(sfin), shift)
    return x, ent


def _layer_sample(x, w, tab, l, page_table, caches, state, shift_prev, final_g):
    n = x.shape[0]
    x = _ffn(x, w["n1"], w["f1i"], w["f1o"])
    sg, pa, qlat, qrope, lat, krope, sbk, sbv, sbq = _proj_sample(x, w["nm"], w, tab)
    sin, pin = _rwkv_prep_sample(pa, shift_prev, w)
    vt = pin[:, :A_WIDTH].reshape(n, A_HEADS, A_HEAD_DIM, 1)
    sat, ypt, sfin = _rwkv_scan(sin.reshape(n, 1, 5 * A_WIDTH), vt, _pad_state(state), bb=4, tt=1)
    ya = _rwkv_post_sample(sat.reshape(n, A_WIDTH), ypt.reshape(n, A_WIDTH), pin, w["ln"])
    olat, osb = _decode_attn(l, page_table, qlat, qrope, sbq, lat, krope, *caches)
    x = _merge(x, sg, ya, olat, osb, w, absorbed=True)
    x = _ffn(x, w["n2"], w["f2i"], w["f2o"], final_g=final_g)
    ent = (lat.reshape(n, 1, KV_RANK), krope.reshape(n, 1, ROPE_DIM),
           sbk.reshape(n, 1, C_HEADS, C_HEAD_DIM), sbv.reshape(n, 1, C_HEADS, C_HEAD_DIM),
           _unpad_state(sfin), pa)
    return x, ent


def kernel(x_prompt, x_sample, cache_mla_latent, cache_mla_krope, cache_sb_k, cache_sb_v,
           state_rwkv, state_rwkv_shift, page_table,
           norm_ffn1, ffn1_w_in, ffn1_w_out, norm_mix, w_mix_in,
           rwkv_mu, rwkv_w0, rwkv_w_up, rwkv_a0, rwkv_a_up, rwkv_g_up, rwkv_k_k, rwkv_k_a,
           rwkv_r_k, rwkv_ln_w, rwkv_ln_b,
           mla_q_norm, mla_w_uq, mla_kv_norm, mla_w_uk, mla_w_uv,
           w_branch_a, w_branch_b, w_branch_c, w_mix_out,
           norm_ffn2, ffn2_w_in, ffn2_w_out, final_norm):
    P = dict(ffn1_w_in=ffn1_w_in, ffn1_w_out=ffn1_w_out, w_mix_in=w_mix_in, rwkv_mu=rwkv_mu,
             rwkv_w0=rwkv_w0, rwkv_w_up=rwkv_w_up, rwkv_a0=rwkv_a0, rwkv_a_up=rwkv_a_up,
             rwkv_g_up=rwkv_g_up, rwkv_k_k=rwkv_k_k, rwkv_k_a=rwkv_k_a, rwkv_r_k=rwkv_r_k,
             rwkv_ln_w=rwkv_ln_w, rwkv_ln_b=rwkv_ln_b, mla_q_norm=mla_q_norm, mla_w_uq=mla_w_uq,
             mla_kv_norm=mla_kv_norm, mla_w_uk=mla_w_uk, mla_w_uv=mla_w_uv, w_branch_a=w_branch_a,
             w_branch_b=w_branch_b, w_branch_c=w_branch_c, w_mix_out=w_mix_out,
             ffn2_w_in=ffn2_w_in, ffn2_w_out=ffn2_w_out)
    depth = norm_ffn1.shape[0]
    bp, tp, d = x_prompt.shape
    n_dec, dec_seq, _ = x_sample.shape
    assert dec_seq == 1, "the sample group carries one new token per sequence"
    past_len = page_table.shape[1] * PAGE_SIZE
    tab_p = _tab_prompt(tp)
    tab_s = _tab_sample(n_dec * dec_seq, dec_seq, past_len)
    xp = x_prompt.reshape(bp * tp, d)
    xs = x_sample.reshape(n_dec * dec_seq, d)
    caches = (cache_mla_latent, cache_mla_krope, cache_sb_k, cache_sb_v)
    new_p, new_s = [], []
    for l in range(depth):
        w = _prep_weights(l, P)
        w["n1"], w["nm"], w["n2"] = norm_ffn1[l], norm_mix[l], norm_ffn2[l]
        final_g = final_norm if l == depth - 1 else None
        xp, ent_p = _layer_prompt(xp, w, tab_p, bp, tp, final_g)
        xs, ent_s = _layer_sample(xs, w, tab_s, l, page_table, caches, state_rwkv[l],
                                  state_rwkv_shift[l], final_g)
        new_p.append(ent_p)
        new_s.append(ent_s)
    outs_p = [jnp.stack(t) for t in zip(*new_p)]
    outs_s = [jnp.stack(t) for t in zip(*new_s)]
    return (xp.reshape(bp, tp, d), xs.reshape(n_dec, dec_seq, d), *outs_p, *outs_s)
```

```python
import functools
import math

import jax
import jax.numpy as jnp
from jax import lax
from jax.experimental import pallas as pl
from jax.experimental.pallas import tpu as pltpu

F32 = jnp.float32
BF16 = jnp.bfloat16

D_MODEL = 1024
PAGE_SIZE = 128
A_HEADS, A_HEAD_DIM = 4, 64
A_WIDTH = A_HEADS * A_HEAD_DIM
W_LORA, A_LORA, G_LORA = 64, 64, 128
RWKV_COLS = 3 * A_WIDTH + W_LORA + A_LORA + G_LORA
GN_EPS = 64e-5
B_HEADS, Q_RANK, KV_RANK, NOPE_DIM, ROPE_DIM, V_DIM = 8, 256, 128, 64, 32, 64
B_WIDTH = B_HEADS * V_DIM
MLA_COLS = Q_RANK + KV_RANK + ROPE_DIM
ROPE_BASE = 10000.0
C_HEADS, C_HEAD_DIM = 4, 64
C_WIDTH = C_HEADS * C_HEAD_DIM
SB_COLS = 3 * C_WIDTH
FFN_DIM = 2816
EPS = 1e-6

LANES = 128
VMEM_LIMIT = 56 * 1024 * 1024

_G0 = 0
_RW0 = 3 * D_MODEL
_MQ0 = _RW0 + RWKV_COLS
_SB0 = _MQ0 + Q_RANK + KV_RANK
_KRA = _SB0 + SB_COLS
_KRB = _KRA + LANES
_WIN_COLS = _KRB + LANES

_NEG = -1e30


def _cparams(*sem):
    return pltpu.CompilerParams(dimension_semantics=sem, vmem_limit_bytes=VMEM_LIMIT)


def _const_spec(shape):
    nd = len(shape)
    return pl.BlockSpec(shape, lambda *_: (0,) * nd, pipeline_mode=pl.Buffered(1))


def _dot(a, b):
    return jnp.dot(a, b, preferred_element_type=F32)


def _dot_nt(a, b):
    return lax.dot_general(a, b, (((1,), (1,)), ((), ())), preferred_element_type=F32)


def _rms(x, g):
    return x * lax.rsqrt(jnp.mean(x * x, axis=-1, keepdims=True) + EPS) * g


def _softplus_parts(z):
    sp = jnp.log1p(jnp.exp(-jnp.abs(z)))
    return jnp.maximum(z, 0.0) + sp, jnp.minimum(z, 0.0) - sp


def _split3(x):
    x1 = x.astype(BF16)
    r1 = x - x1.astype(F32)
    x2 = r1.astype(BF16)
    x3 = (r1 - x2.astype(F32)).astype(BF16)
    return x1, x2, x3


def _exact_dot01(x, m):
    x1, x2, x3 = _split3(x)
    return _dot(x1, m) + _dot(x2, m) + _dot(x3, m)


def _head_ones(width, head):
    r = lax.broadcasted_iota(jnp.int32, (width, width), 0) // head
    c = lax.broadcasted_iota(jnp.int32, (width, width), 1) // head
    return jnp.where(r == c, 1.0, 0.0).astype(BF16)


def _ffn_kernel(*refs, chunks, final):
    if final:
        x_ref, g_ref, win_ref, wout_ref, gf_ref, o_ref = refs
    else:
        x_ref, g_ref, win_ref, wout_ref, o_ref = refs
    x = x_ref[...]
    hb = _rms(x, g_ref[...]).astype(BF16)
    acc = None
    for a, b in chunks:
        gt = _dot(hb, win_ref[:, a:b])
        up = _dot(hb, win_ref[:, FFN_DIM + a:FFN_DIM + b])
        act = (gt * jax.nn.sigmoid(gt) * up).astype(BF16)
        part = _dot(act, wout_ref[a:b, :])
        acc = part if acc is None else acc + part
    y = x + 0.5 * acc
    if final:
        y = _rms(y, gf_ref[...])
    o_ref[...] = y


def _ffn(x, g, win_b, wout_b, final_g=None, tm=512):
    n, d = x.shape
    tm = min(tm, n)
    half = FFN_DIM // 2
    chunks = ((0, half), (half, FFN_DIM))
    in_specs = [pl.BlockSpec((tm, d), lambda i: (i, 0)), _const_spec((1, d)),
                _const_spec(win_b.shape), _const_spec(wout_b.shape)]
    args = [x, g.reshape(1, d), win_b, wout_b]
    if final_g is not None:
        in_specs.append(_const_spec((1, d)))
        args.append(final_g.reshape(1, d))
    return pl.pallas_call(
        functools.partial(_ffn_kernel, chunks=chunks, final=final_g is not None),
        grid=(n // tm,),
        in_specs=in_specs,
        out_specs=pl.BlockSpec((tm, d), lambda i: (i, 0)),
        out_shape=jax.ShapeDtypeStruct((n, d), F32),
        compiler_params=_cparams("parallel"),
    )(*args)


def _proj_common(x_ref, g_ref, win_ref, sg_ref, pa_ref):
    hb = _rms(x_ref[...], g_ref[...]).astype(BF16)
    sg_ref[...] = jax.nn.sigmoid(_dot(hb, win_ref[:, _G0:_RW0]))
    pa_ref[...] = _dot(hb, win_ref[:, _RW0:_MQ0])
    cm = _dot(hb, win_ref[:, _MQ0:_SB0])
    sb = _dot(hb, win_ref[:, _SB0:_KRA])
    kra = _dot(hb, win_ref[:, _KRA:_KRB])
    krb = _dot(hb, win_ref[:, _KRB:_WIN_COLS])
    return cm[:, :Q_RANK], cm[:, Q_RANK:], sb, kra, krb


def _proj_prompt_kernel(x_ref, g_ref, win_ref, tab_ref, qn_ref, kvn_ref, wuq_ref, wuk_ref, wuv_ref,
                        sg_ref, pa_ref, qf_ref, kcat_ref, vv_ref, lat_ref, krope_ref,
                        sbk_ref, sbv_ref, sbqkv_ref):
    cq, ckv, sb, kra, krb = _proj_common(x_ref, g_ref, win_ref, sg_ref, pa_ref)
    sbk_ref[...] = sb[:, C_WIDTH:2 * C_WIDTH]
    sbv_ref[...] = sb[:, 2 * C_WIDTH:]
    sbqkv_ref[...] = sb.astype(BF16)
    tab = tab_ref[...]
    kr = kra * tab[:, 2 * LANES:3 * LANES] + krb * tab[:, 3 * LANES:]
    krope_ref[...] = kr[:, :ROPE_DIM]
    lat = _rms(ckv, kvn_ref[...])
    lat_ref[...] = lat
    latb = lat.astype(BF16)
    hq = _rms(cq, qn_ref[...]).astype(BF16)
    q1 = _dot(hq, wuq_ref[0])
    q2 = _dot(hq, wuq_ref[1])
    kn = _dot(latb, wuk_ref[...])
    kadd = pltpu.roll(kr, NOPE_DIM, 1)
    cq_t, sq_t = tab[:, :LANES], tab[:, LANES:2 * LANES]
    for h in range(B_HEADS):
        sl = slice(h * LANES, (h + 1) * LANES)
        qf_ref[:, sl] = (q1[:, sl] * cq_t + q2[:, sl] * sq_t).astype(BF16)
        kcat_ref[:, sl] = (kn[:, sl] + kadd).astype(BF16)
    vv_ref[...] = _dot(latb, wuv_ref[...]).astype(BF16)


def _proj_sample_kernel(x_ref, g_ref, win_ref, tab_ref, qn_ref, kvn_ref, wuq_ref, wukt_ref,
                        sg_ref, pa_ref, qlat_ref, qrope_ref, lat_ref, krope_ref,
                        sbk_ref, sbv_ref, sbq_ref):
    cq, ckv, sb, kra, krb = _proj_common(x_ref, g_ref, win_ref, sg_ref, pa_ref)
    sbq_ref[...] = sb[:, :C_WIDTH].astype(BF16)
    sbk_ref[...] = sb[:, C_WIDTH:2 * C_WIDTH]
    sbv_ref[...] = sb[:, 2 * C_WIDTH:]
    tab = tab_ref[...]
    rw = B_HEADS * ROPE_DIM
    kr = kra * tab[:, 2 * rw:2 * rw + LANES] + krb * tab[:, 2 * rw + LANES:]
    krope_ref[...] = kr[:, :ROPE_DIM]
    lat_ref[...] = _rms(ckv, kvn_ref[...])
    hq = _rms(cq, qn_ref[...]).astype(BF16)
    qs = _dot(hq, wuq_ref[...])
    nw = B_HEADS * NOPE_DIM
    qn = qs[:, :nw].astype(BF16)
    qrope_ref[...] = (qs[:, nw:nw + rw] * tab[:, :rw] + qs[:, nw + rw:] * tab[:, rw:2 * rw]).astype(BF16)
    for h in range(B_HEADS):
        p = h // 2
        qlat_ref[:, h * LANES:(h + 1) * LANES] = _dot(
            qn[:, p * LANES:(p + 1) * LANES], wukt_ref[h]).astype(BF16)


def _proj_prompt(x, g, w, tab, seq, tm=256):
    n, d = x.shape
    nt = seq // tm
    row = lambda c: pl.BlockSpec((tm, c), lambda i: (i, 0))
    outs = [(3 * D_MODEL, F32), (RWKV_COLS, F32), (B_HEADS * LANES, BF16), (B_HEADS * LANES, BF16),
            (B_WIDTH, BF16), (KV_RANK, F32), (ROPE_DIM, F32), (C_WIDTH, F32), (C_WIDTH, F32),
            (SB_COLS, BF16)]
    return pl.pallas_call(
        _proj_prompt_kernel,
        grid=(n // tm,),
        in_specs=[row(d), _const_spec((1, d)), _const_spec(w["win"].shape),
                  pl.BlockSpec((tm, 4 * LANES), lambda i: (i % nt, 0)),
                  _const_spec((1, Q_RANK)), _const_spec((1, KV_RANK)),
                  _const_spec(w["wuq_p"].shape), _const_spec(w["wuk_p"].shape),
                  _const_spec(w["wuv"].shape)],
        out_specs=[row(c) for c, _ in outs],
        out_shape=[jax.ShapeDtypeStruct((n, c), dt) for c, dt in outs],
        compiler_params=_cparams("parallel"),
    )(x, g.reshape(1, d), w["win"], tab, w["qn"], w["kvn"], w["wuq_p"], w["wuk_p"], w["wuv"])


def _proj_sample(x, g, w, tab):
    n, d = x.shape
    tm = n
    row = lambda c: pl.BlockSpec((tm, c), lambda i: (i, 0))
    outs = [(3 * D_MODEL, F32), (RWKV_COLS, F32), (B_HEADS * LANES, BF16), (B_HEADS * ROPE_DIM, BF16),
            (KV_RANK, F32), (ROPE_DIM, F32), (C_WIDTH, F32), (C_WIDTH, F32), (C_WIDTH, BF16)]
    return pl.pallas_call(
        _proj_sample_kernel,
        grid=(n // tm,),
        in_specs=[row(d), _const_spec((1, d)), _const_spec(w["win"].shape), row(tab.shape[1]),
                  _const_spec((1, Q_RANK)), _const_spec((1, KV_RANK)),
                  _const_spec(w["wuq_s"].shape), _const_spec(w["wukt"].shape)],
        out_specs=[row(c) for c, _ in outs],
        out_shape=[jax.ShapeDtypeStruct((n, c), dt) for c, dt in outs],
        compiler_params=_cparams("parallel"),
    )(x, g.reshape(1, d), w["win"], tab, w["qn"], w["kvn"], w["wuq_s"], w["wukt"])


def _rwkv_prep_kernel(*refs, prompt):
    if prompt:
        (pa_ref, prev_ref, mu_ref, vec_ref, wup_ref, aup_ref, gup_ref,
         sin_ref, pin_ref, vt_ref) = refs
        p = pa_ref[...]
        tt = p.shape[0]
        first = pl.program_id(1) == 0
        prev_row = jnp.where(first, 0.0, prev_ref[7:8, :])
        rows = lax.broadcasted_iota(jnp.int32, p.shape, 0)
        p_prev = jnp.where(rows == 0, prev_row, pltpu.roll(p, 1, 0))
    else:
        (pa_ref, prev_ref, mu_ref, vec_ref, wup_ref, aup_ref, gup_ref,
         sin_ref, pin_ref) = refs
        p = pa_ref[...]
        p_prev = prev_ref[...]
    xm = p + (p_prev - p) * mu_ref[...]
    aw = A_WIDTH
    r, k, v = xm[:, :aw], xm[:, aw:2 * aw], xm[:, 2 * aw:3 * aw]
    wa = xm[:, 3 * aw:3 * aw + LANES]
    gd = xm[:, 3 * aw + LANES:]
    vec = vec_ref[...]
    w0, a0, k_k, k_a, r_k = (vec[i:i + 1, :] for i in range(5))
    lw = _dot(jnp.tanh(wa).astype(BF16), wup_ref[...])
    la = _dot(wa.astype(BF16), aup_ref[...])
    sp, _ = _softplus_parts(-(w0 + lw))
    decay = jnp.exp(-jnp.exp(-sp - 0.5))
    a = jax.nn.sigmoid(a0 + la)
    g = _dot(jax.nn.sigmoid(gd).astype(BF16), gup_ref[...])
    ones = _head_ones(aw, A_HEAD_DIM)
    kk = k * k_k
    kk = kk / jnp.maximum(jnp.sqrt(_exact_dot01(kk * kk, ones)), 1e-12)
    km = k * (1.0 + (a - 1.0) * k_a)
    b = kk * a
    sin_ref[:, 0 * aw:1 * aw] = decay
    sin_ref[:, 1 * aw:2 * aw] = -kk
    sin_ref[:, 2 * aw:3 * aw] = b
    sin_ref[:, 3 * aw:4 * aw] = km
    sin_ref[:, 4 * aw:5 * aw] = decay * r
    pin_ref[:, 0 * aw:1 * aw] = v
    pin_ref[:, 1 * aw:2 * aw] = _exact_dot01(b * r, ones)
    pin_ref[:, 2 * aw:3 * aw] = _exact_dot01(km * r, ones)
    pin_ref[:, 3 * aw:4 * aw] = _exact_dot01(r * km * r_k, ones)
    pin_ref[:, 4 * aw:5 * aw] = g
    if prompt:
        for p in range(A_HEADS // 2):
            for blk in range(tt // BLK):
                vt_ref[p, blk] = _pair_transpose(v[blk * BLK:(blk + 1) * BLK, p * LANES:(p + 1) * LANES])


def _rwkv_prep_prompt(pa, w, batch, seq, tt=256):
    n = pa.shape[0]
    nt = seq // tt
    pa3 = pa.reshape(batch, seq, RWKV_COLS)
    aw5 = 5 * A_WIDTH
    sin, pin, vt = pl.pallas_call(
        functools.partial(_rwkv_prep_kernel, prompt=True),
        grid=(batch, nt),
        in_specs=[pl.BlockSpec((None, tt, RWKV_COLS), lambda b, i: (b, i, 0)),
                  pl.BlockSpec((None, 8, RWKV_COLS), lambda b, i: (b, jnp.maximum(i * (tt // 8) - 1, 0), 0)),
                  _const_spec((1, RWKV_COLS)), _const_spec((8, A_WIDTH)),
                  _const_spec((LANES, A_WIDTH)), _const_spec((LANES, A_WIDTH)),
                  _const_spec((G_LORA, A_WIDTH))],
        out_specs=[pl.BlockSpec((None, tt, aw5), lambda b, i: (b, i, 0)),
                   pl.BlockSpec((None, tt, aw5), lambda b, i: (b, i, 0)),
                   pl.BlockSpec((None, A_HEADS // 2, tt // BLK, A_HEAD_DIM, LANES),
                                lambda b, i: (b, 0, i, 0, 0))],
        out_shape=[jax.ShapeDtypeStruct((batch, seq, aw5), F32),
                   jax.ShapeDtypeStruct((batch, seq, aw5), F32),
                   jax.ShapeDtypeStruct((batch, A_HEADS // 2, seq // BLK, A_HEAD_DIM, LANES), F32)],
        compiler_params=_cparams("parallel", "parallel"),
    )(pa3, pa3, w["mu"], w["rvec"], w["wup"], w["aup"], w["gup"])
    return sin, pin, vt


def _rwkv_prep_sample(pa, shift, w):
    n = pa.shape[0]
    aw5 = 5 * A_WIDTH
    full = lambda c: pl.BlockSpec((n, c), lambda i: (0, 0))
    sin, pin = pl.pallas_call(
        functools.partial(_rwkv_prep_kernel, prompt=False),
        grid=(1,),
        in_specs=[full(RWKV_COLS), full(RWKV_COLS),
                  _const_spec((1, RWKV_COLS)), _const_spec((8, A_WIDTH)),
                  _const_spec((LANES, A_WIDTH)), _const_spec((LANES, A_WIDTH)),
                  _const_spec((G_LORA, A_WIDTH))],
        out_specs=[full(aw5), full(aw5)],
        out_shape=[jax.ShapeDtypeStruct((n, aw5), F32), jax.ShapeDtypeStruct((n, aw5), F32)],
        compiler_params=_cparams("arbitrary"),
    )(pa, shift, w["mu"], w["rvec"], w["wup"], w["aup"], w["gup"])
    return sin, pin


BLK = 64


def _pair_transpose(x):
    z = jnp.concatenate([x, pltpu.roll(x, A_HEAD_DIM, 1)], axis=0)
    return z.T[:A_HEAD_DIM, :]


def _rwkv_scan_kernel(sin_ref, vt_ref, sa_ref, yp_ref, sfin_ref, s_ref, csa_ref, cyp_ref, *, bb, grp):
    tb = pl.program_id(1)

    @pl.when(tb == 0)
    def _():
        s_ref[...] = jnp.zeros_like(s_ref)

    aw = A_WIDTH
    ones = _head_ones(2 * LANES, A_HEAD_DIM)
    lane = lax.broadcasted_iota(jnp.int32, (A_HEAD_DIM, LANES), 1)
    half = (lane // A_HEAD_DIM) * A_HEAD_DIM
    step_lane = lane - half
    pairs = [(b, p) for b in range(bb) for p in range(A_HEADS // 2)]
    sub = 8

    def body(tg, carry):
        t0 = pl.multiple_of(tg * sub, sub)
        tiles = {}
        for b, p in pairs:
            tiles[b, p] = [sin_ref[b, pl.ds(t0, sub), q * aw + p * LANES:q * aw + (p + 1) * LANES]
                           for q in range(5)]
        for u in range(sub):
            sel = step_lane == t0 + u
            idx = half + (t0 + u)
            for g0 in range(0, len(pairs), grp):
                group = pairs[g0:g0 + grp]
                his, los = [], []
                for bp in group:
                    s = s_ref[bp]
                    w, kkn, bv, km, wr = (x[u:u + 1, :] for x in tiles[bp])
                    f = jnp.concatenate([s * kkn, s * wr], axis=1)
                    hi = f.astype(BF16)
                    his.append(hi)
                    los.append((f - hi.astype(F32)).astype(BF16))
                red = _dot(jnp.concatenate(his + los, axis=0), ones)
                for k, bp in enumerate(group):
                    w, kkn, bv, km, wr = (x[u:u + 1, :] for x in tiles[bp])
                    r = (red[k * A_HEAD_DIM:(k + 1) * A_HEAD_DIM]
                         + red[(grp + k) * A_HEAD_DIM:(grp + k + 1) * A_HEAD_DIM])
                    sa, yp = r[:, :LANES], r[:, LANES:]
                    vc = jnp.take_along_axis(vt_ref[bp], idx, axis=1)
                    s_ref[bp] = s_ref[bp] * w + sa * bv + vc * km
                    pltpu.store(csa_ref.at[bp], sa, mask=sel)
                    pltpu.store(cyp_ref.at[bp], yp, mask=sel)
        return carry

    lax.fori_loop(0, BLK // sub, body, 0)
    for b, p in pairs:
        sa_ref[b, :, p * LANES:(p + 1) * LANES] = _pair_transpose(csa_ref[b, p])
        yp_ref[b, :, p * LANES:(p + 1) * LANES] = _pair_transpose(cyp_ref[b, p])

    @pl.when(tb == pl.num_programs(1) - 1)
    def _():
        sfin_ref[...] = s_ref[...]


def _rwkv_scan(sin, vt2, bb=8, grp=4):
    batch, seq, aw5 = sin.shape
    npair = A_HEADS // 2
    rows = pl.BlockSpec((bb, BLK, A_WIDTH), lambda g, i: (g, i, 0))
    sblk = pl.BlockSpec((bb, npair, A_HEAD_DIM, LANES), lambda g, i: (g, 0, 0, 0))
    tile = pltpu.VMEM((bb, npair, A_HEAD_DIM, LANES), F32)
    return pl.pallas_call(
        functools.partial(_rwkv_scan_kernel, bb=bb, grp=grp),
        grid=(batch // bb, seq // BLK),
        in_specs=[pl.BlockSpec((bb, BLK, aw5), lambda g, i: (g, i, 0)),
                  pl.BlockSpec((bb, npair, None, A_HEAD_DIM, LANES), lambda g, i: (g, 0, i, 0, 0))],
        out_specs=[rows, rows, sblk],
        out_shape=[jax.ShapeDtypeStruct((batch, seq, A_WIDTH), F32),
                   jax.ShapeDtypeStruct((batch, seq, A_WIDTH), F32),
                   jax.ShapeDtypeStruct((batch, npair, A_HEAD_DIM, LANES), F32)],
        scratch_shapes=[tile, tile, tile],
        compiler_params=_cparams("parallel", "arbitrary"),
    )(sin, vt2)


def _rwkv_step_kernel(sin_ref, v_ref, s0_ref, sa_ref, yp_ref, sfin_ref, *, bb):
    aw = A_WIDTH
    lane_j = lax.broadcasted_iota(jnp.int32, (1, LANES), 1)
    for b in range(bb):
        for h in range(A_HEADS):
            p = h // 2
            own = (lane_j // A_HEAD_DIM) == (h % 2)
            rowv = lambda q: sin_ref[b, :, q * aw + p * LANES:q * aw + (p + 1) * LANES]
            s = s0_ref[b, h]
            sa = jnp.sum(s * rowv(1), axis=1, keepdims=True)
            yp = jnp.sum(s * rowv(4), axis=1, keepdims=True)
            sfin_ref[b, h] = (s * rowv(0) + sa * jnp.where(own, rowv(2), 0.0)
                              + v_ref[b, h] * jnp.where(own, rowv(3), 0.0))
            sa_ref[b, h] = sa
            yp_ref[b, h] = yp


def _rwkv_step(sin, v, s0p, bb=4):
    n, aw5 = sin.shape
    col = pl.BlockSpec((bb, A_HEADS, A_HEAD_DIM, 1), lambda g: (g, 0, 0, 0))
    sblk = pl.BlockSpec((bb, A_HEADS, A_HEAD_DIM, LANES), lambda g: (g, 0, 0, 0))
    col_shape = jax.ShapeDtypeStruct((n, A_HEADS, A_HEAD_DIM, 1), F32)
    sa, yp, sfin = pl.pallas_call(
        functools.partial(_rwkv_step_kernel, bb=bb),
        grid=(n // bb,),
        in_specs=[pl.BlockSpec((bb, 1, aw5), lambda g: (g, 0, 0)), col, sblk],
        out_specs=[col, col, sblk],
        out_shape=[col_shape, col_shape, jax.ShapeDtypeStruct((n, A_HEADS, A_HEAD_DIM, LANES), F32)],
        compiler_params=_cparams("parallel"),
    )(sin.reshape(n, 1, aw5), v.reshape(n, A_HEADS, A_HEAD_DIM, 1), s0p)
    return sa.reshape(n, A_WIDTH), yp.reshape(n, A_WIDTH), sfin


def _rwkv_post_kernel(sa_ref, yp_ref, pin_ref, ln_ref, ya_ref):
    aw = A_WIDTH
    sa, yp = sa_ref[...], yp_ref[...]
    pin = pin_ref[...]
    v, brb, krb, bc, g = (pin[:, i * aw:(i + 1) * aw] for i in range(5))
    y = yp + sa * brb + v * krb
    ones = _head_ones(aw, A_HEAD_DIM)
    inv = 1.0 / A_HEAD_DIM
    mean = _exact_dot01(y, ones) * inv
    dlt = y - mean
    var = _exact_dot01(dlt * dlt, ones) * inv
    yn = dlt * lax.rsqrt(var + GN_EPS) * ln_ref[0:1, :] + ln_ref[1:2, :]
    ya_ref[...] = ((yn + bc * v) * g).astype(BF16)


def _rwkv_post(sa, yp, pin, ln, tm=512):
    n = pin.shape[0]
    tm = min(tm, n)
    row = lambda c: pl.BlockSpec((tm, c), lambda i: (i, 0))
    return pl.pallas_call(
        _rwkv_post_kernel,
        grid=(n // tm,),
        in_specs=[row(A_WIDTH), row(A_WIDTH), row(5 * A_WIDTH), _const_spec((8, A_WIDTH))],
        out_specs=row(A_WIDTH),
        out_shape=jax.ShapeDtypeStruct((n, A_WIDTH), BF16),
        compiler_params=_cparams("parallel"),
    )(sa, yp, pin, ln)


def _mla_prompt_kernel(q_ref, k_ref, v_ref, o_ref, *, tq, scale, hpl):
    i = pl.program_id(1)
    row = lax.broadcasted_iota(jnp.int32, (tq, tq), 0)
    col = lax.broadcasted_iota(jnp.int32, (tq, tq), 1)
    causal = col <= row
    lane = lax.broadcasted_iota(jnp.int32, (tq, LANES), 1)
    c2 = scale * math.log2(math.e)
    for h0 in range(0, B_HEADS, hpl):
        heads = list(range(h0, h0 + hpl))
        qs = [q_ref[:, h * LANES:(h + 1) * LANES] for h in heads]

        def step(j, carry, diag, heads=heads, qs=qs):
            off = pl.multiple_of(j * tq, tq)
            out = []
            for n, h in enumerate(heads):
                m, l, acc = carry[n]
                s = _dot_nt(qs[n], k_ref[pl.ds(off, tq), h * LANES:(h + 1) * LANES])
                if diag:
                    s = jnp.where(causal, s, _NEG)
                m_new = jnp.maximum(m, jnp.max(s, axis=-1, keepdims=True))
                alpha = jnp.exp2((m - m_new) * c2)
                pr = jnp.exp2((s - m_new) * c2)
                l = alpha * l + jnp.sum(pr, axis=-1, keepdims=True)
                v = v_ref[pl.ds(off, tq), (h // 2) * LANES:(h // 2 + 1) * LANES]
                out.append((m_new, l, alpha * acc + _dot(pr.astype(BF16), v)))
            return tuple(out)

        init = tuple((jnp.full((tq, 1), _NEG, F32), jnp.zeros((tq, 1), F32), jnp.zeros((tq, LANES), F32))
                     for _ in heads)
        carry = lax.fori_loop(0, i, functools.partial(step, diag=False), init)
        res = [acc / l for _, l, acc in step(i, carry, True)]
        for n in range(0, hpl, 2):
            p = heads[n] // 2
            o_ref[:, p * LANES:(p + 1) * LANES] = jnp.where(lane < V_DIM, res[n], res[n + 1]).astype(BF16)


def _mla_prompt(qf, kcat, vv, batch, seq, tq=256, hpl=B_HEADS):
    hw = B_HEADS * LANES
    scale = 1.0 / math.sqrt(NOPE_DIM + ROPE_DIM)
    return pl.pallas_call(
        functools.partial(_mla_prompt_kernel, tq=tq, scale=scale, hpl=hpl),
        grid=(batch, seq // tq),
        in_specs=[pl.BlockSpec((None, tq, hw), lambda b, i: (b, i, 0)),
                  pl.BlockSpec((None, seq, hw), lambda b, i: (b, 0, 0)),
                  pl.BlockSpec((None, seq, B_WIDTH), lambda b, i: (b, 0, 0))],
        out_specs=pl.BlockSpec((None, tq, B_WIDTH), lambda b, i: (b, i, 0)),
        out_shape=jax.ShapeDtypeStruct((batch, seq, B_WIDTH), BF16),
        compiler_params=_cparams("parallel", "arbitrary"),
    )(qf.reshape(batch, seq, hw), kcat.reshape(batch, seq, hw), vv.reshape(batch, seq, B_WIDTH))


def _sb_weights(z, c, upper, valid):
    soft, logsig = _softplus_parts(z)
    if valid is not None:
        soft = jnp.where(valid, soft, 0.0)
    hi = soft.astype(BF16)
    lo = (soft - hi.astype(F32)).astype(BF16)
    later = _dot(hi, upper) + _dot(lo, upper)
    a = jnp.exp(logsig - later - c)
    if valid is not None:
        a = jnp.where(valid, a, 0.0)
    return a, c + jnp.sum(soft, axis=-1, keepdims=True)


def _upper_ones(tk):
    r = lax.broadcasted_iota(jnp.int32, (tk, tk), 0)
    c = lax.broadcasted_iota(jnp.int32, (tk, tk), 1)
    return jnp.where(r > c, 1.0, 0.0).astype(BF16)


def _sb_prompt_kernel(q_ref, k_ref, v_ref, o_ref, *, tq, scale):
    i = pl.program_id(1)
    row = lax.broadcasted_iota(jnp.int32, (tq, tq), 0)
    col = lax.broadcasted_iota(jnp.int32, (tq, tq), 1)
    strict = col < row
    upper = _upper_ones(tq)
    lane = lax.broadcasted_iota(jnp.int32, (tq, LANES), 1)
    qms = []
    for h in range(C_HEADS):
        qp = q_ref[:, (h // 2) * LANES:(h // 2 + 1) * LANES].astype(F32)
        qms.append(jnp.where((lane // C_HEAD_DIM) == (h % 2), qp, 0.0).astype(BF16))

    def step(j, carry, diag):
        off = pl.multiple_of(j * tq, tq)
        out = []
        for h in range(C_HEADS):
            c, acc = carry[h]
            psl = slice((h // 2) * LANES, (h // 2 + 1) * LANES)
            z = _dot_nt(qms[h], k_ref[pl.ds(off, tq), psl]) * scale
            a, c = _sb_weights(z, c, upper, strict if diag else None)
            out.append((c, acc + _dot(a.astype(BF16), v_ref[pl.ds(off, tq), psl])))
        return tuple(out)

    init = tuple((jnp.zeros((tq, 1), F32), jnp.zeros((tq, LANES), F32)) for _ in range(C_HEADS))
    carry = step(i, init, True)
    res = lax.fori_loop(0, i, lambda jj, cr: step(i - 1 - jj, cr, False), carry)
    for p in range(C_HEADS // 2):
        o_ref[:, p * LANES:(p + 1) * LANES] = jnp.where(
            lane < C_HEAD_DIM, res[2 * p][1], res[2 * p + 1][1]).astype(BF16)


def _sb_prompt(sbqkv, batch, seq, tq=256):
    x = sbqkv.reshape(batch, seq, SB_COLS)
    scale = C_HEAD_DIM ** -0.5
    return pl.pallas_call(
        functools.partial(_sb_prompt_kernel, tq=tq, scale=scale),
        grid=(batch, seq // tq),
        in_specs=[pl.BlockSpec((None, tq, C_WIDTH), lambda b, i: (b, i, 0)),
                  pl.BlockSpec((None, seq, C_WIDTH), lambda b, i: (b, 0, 1)),
                  pl.BlockSpec((None, seq, C_WIDTH), lambda b, i: (b, 0, 2))],
        out_specs=pl.BlockSpec((None, tq, C_WIDTH), lambda b, i: (b, i, 0)),
        out_shape=jax.ShapeDtypeStruct((batch, seq, C_WIDTH), BF16),
        compiler_params=_cparams("parallel", "arbitrary"),
    )(x, x, x)


def _decode_kernel(pt_ref, qlat_ref, qrope_ref, sbq_ref, latn_ref, krn_ref, *rest, npg, steps, scale_b, scale_c):
    pages = rest[:4 * npg]
    olat_ref, osb_ref, s_scr, z_scr, acc_ref, accs_ref, stat_ref, lat_scr = rest[4 * npg:]
    s_idx = pl.program_id(1)
    rows = 8
    lane_c = lax.broadcasted_iota(jnp.int32, (rows, C_WIDTH), 1)
    row_c = lax.broadcasted_iota(jnp.int32, (rows, C_WIDTH), 0)
    own = (lane_c // C_HEAD_DIM) == row_c

    @pl.when(s_idx < steps)
    def _():
        ql, qr = qlat_ref[...], qrope_ref[...]
        qm = jnp.where(own, jnp.broadcast_to(sbq_ref[...].astype(F32), own.shape), 0.0).astype(BF16)
        for g in range(npg):
            lat_ref, krt_ref, kt_ref, _ = pages[4 * g:4 * g + 4]
            pg = s_idx * npg + g
            latb = lat_ref[...].astype(BF16)
            lat_scr[pg] = latb
            s_scr[pg] = (_dot_nt(ql, latb) + _dot(qr, krt_ref[...].astype(BF16))) * scale_b
            z_scr[pg] = _dot(qm, kt_ref[...].astype(BF16)) * scale_c

    @pl.when(s_idx == steps - 1)
    def _():
        latn = latn_ref[...].astype(BF16).astype(F32)
        krn = krn_ref[...].astype(BF16).astype(F32)
        s_self = (jnp.sum(qlat_ref[...].astype(F32) * latn, axis=-1, keepdims=True)
                  + jnp.sum(qrope_ref[...].astype(F32) * krn, axis=-1, keepdims=True)) * scale_b
        s = s_scr[...]
        m = jnp.maximum(jnp.max(jnp.max(s, axis=0), axis=-1, keepdims=True), s_self)
        pr = jnp.exp(s - m)
        p_self = jnp.exp(s_self - m)
        l = jnp.sum(jnp.sum(pr, axis=0), axis=-1, keepdims=True) + p_self
        s_scr[...] = pr
        stat_ref[0] = jnp.broadcast_to(l, (rows, LANES))
        stat_ref[1] = jnp.broadcast_to(p_self, (rows, LANES))
        npages = z_scr.shape[0]
        soft, logsig = _softplus_parts(z_scr[...])
        flat = soft.reshape(npages * rows, PAGE_SIZE)
        hi = flat.astype(BF16)
        lo = (flat - hi.astype(F32)).astype(BF16)
        upper = _upper_ones(PAGE_SIZE)
        later = (_dot(hi, upper) + _dot(lo, upper)).reshape(soft.shape)
        tot = jnp.broadcast_to(jnp.sum(soft, axis=-1, keepdims=True), soft.shape)
        suf = tot
        k = 1
        while k < npages:
            suf = suf + jnp.concatenate([suf[k:], jnp.zeros((k,) + suf.shape[1:], F32)], axis=0)
            k *= 2
        z_scr[...] = jnp.exp(logsig - later - (suf - tot))
        acc_ref[...] = jnp.zeros_like(acc_ref)
        accs_ref[...] = jnp.zeros_like(accs_ref)

    @pl.when(s_idx >= steps)
    def _():
        acc, accs = acc_ref[...], accs_ref[...]
        for g in range(npg):
            vt_ref = pages[4 * g + 3]
            pg = (s_idx - steps) * npg + g
            acc = acc + _dot(s_scr[pg].astype(BF16), lat_scr[pg])
            accs = accs + _dot_nt(z_scr[pg].astype(BF16), vt_ref[...].astype(BF16))
        acc_ref[...] = acc
        accs_ref[...] = accs

    @pl.when(s_idx == 2 * steps - 1)
    def _():
        latn = latn_ref[...].astype(BF16).astype(F32)
        olat_ref[...] = (acc_ref[...] + stat_ref[1] * latn) / stat_ref[0]
        osb_ref[...] = jnp.sum(jnp.where(own, accs_ref[...], 0.0), axis=0, keepdims=True)


def _decode_attn(layer, page_table, qlat, qrope, sbq, lat_new, kr_new, c_lat, c_kr, c_k, c_v, npg=16):
    n, n_pages = page_table.shape
    c_krt = jnp.swapaxes(c_kr, 2, 3)
    c_kt = jnp.transpose(c_k, (0, 1, 3, 4, 2)).reshape(c_k.shape[:2] + (C_WIDTH, PAGE_SIZE))
    c_vt = jnp.transpose(c_v, (0, 1, 3, 4, 2)).reshape(c_v.shape[:2] + (C_WIDTH, PAGE_SIZE))
    steps = n_pages // npg

    def page_spec(arr, g, first_step):
        def imap(b, s, pt):
            j = jnp.clip(s - first_step, 0, steps - 1)
            return (layer, pt[b, j * npg + g], 0, 0)
        return pl.BlockSpec((None, None) + arr.shape[2:], imap)

    tok = lambda r, c: pl.BlockSpec((None, r, c), lambda b, s, pt: (b, 0, 0))
    in_specs = [tok(B_HEADS, KV_RANK), tok(B_HEADS, ROPE_DIM), tok(1, C_WIDTH), tok(1, KV_RANK),
                tok(1, ROPE_DIM)]
    args = [qlat.reshape(n, B_HEADS, KV_RANK), qrope.reshape(n, B_HEADS, ROPE_DIM),
            sbq.reshape(n, 1, C_WIDTH), lat_new.reshape(n, 1, KV_RANK), kr_new.reshape(n, 1, ROPE_DIM)]
    for g in range(npg):
        in_specs += [page_spec(c_lat, g, 0), page_spec(c_krt, g, 0), page_spec(c_kt, g, 0),
                     page_spec(c_vt, g, steps)]
        args += [c_lat, c_krt, c_kt, c_vt]
    grid_spec = pltpu.PrefetchScalarGridSpec(
        num_scalar_prefetch=1,
        grid=(n, 2 * steps),
        in_specs=in_specs,
        out_specs=[tok(B_HEADS, KV_RANK), tok(1, C_WIDTH)],
        scratch_shapes=[pltpu.VMEM((n_pages, 8, PAGE_SIZE), F32), pltpu.VMEM((n_pages, 8, PAGE_SIZE), F32),
                        pltpu.VMEM((8, KV_RANK), F32), pltpu.VMEM((8, C_WIDTH), F32),
                        pltpu.VMEM((2, 8, LANES), F32),
                        pltpu.VMEM((n_pages, PAGE_SIZE, KV_RANK), BF16)],
    )
    olat, osb = pl.pallas_call(
        functools.partial(_decode_kernel, npg=npg, steps=steps,
                          scale_b=1.0 / math.sqrt(NOPE_DIM + ROPE_DIM), scale_c=C_HEAD_DIM ** -0.5),
        grid_spec=grid_spec,
        out_shape=[jax.ShapeDtypeStruct((n, B_HEADS, KV_RANK), F32),
                   jax.ShapeDtypeStruct((n, 1, C_WIDTH), F32)],
        compiler_params=_cparams("parallel", "arbitrary"),
    )(page_table, *args)
    return olat.reshape(n, B_HEADS * KV_RANK), osb.reshape(n, C_WIDTH)


def _merge_kernel(*refs, absorbed):
    if absorbed:
        x_ref, sg_ref, ya_ref, yb_ref, yc_ref, wuv_ref, wba_ref, wbb_ref, wbc_ref, wo_ref, o_ref = refs
        yb = _dot(yb_ref[...].astype(BF16), wuv_ref[...]).astype(BF16)
    else:
        x_ref, sg_ref, ya_ref, yb_ref, yc_ref, wba_ref, wbb_ref, wbc_ref, wo_ref, o_ref = refs
        yb = yb_ref[...]
    d = D_MODEL
    merged = (sg_ref[:, :d] * _dot(ya_ref[...], wba_ref[...])
              + sg_ref[:, d:2 * d] * _dot(yb, wbb_ref[...])
              + sg_ref[:, 2 * d:] * _dot(yc_ref[...].astype(BF16), wbc_ref[...]))
    o_ref[...] = x_ref[...] + _dot(merged.astype(BF16), wo_ref[...])


def _merge(x, sg, ya, yb, yc, w, absorbed, tm=512):
    n, d = x.shape
    tm = min(tm, n)
    row = lambda c: pl.BlockSpec((tm, c), lambda i: (i, 0))
    in_specs = [row(d), row(3 * d), row(A_WIDTH), row(yb.shape[1]), row(C_WIDTH)]
    args = [x, sg, ya, yb, yc]
    if absorbed:
        in_specs.append(_const_spec(w["wuv_bd"].shape))
        args.append(w["wuv_bd"])
    for name in ("wba", "wbb", "wbc", "wo"):
        in_specs.append(_const_spec(w[name].shape))
        args.append(w[name])
    return pl.pallas_call(
        functools.partial(_merge_kernel, absorbed=absorbed),
        grid=(n // tm,),
        in_specs=in_specs,
        out_specs=row(d),
        out_shape=jax.ShapeDtypeStruct((n, d), F32),
        compiler_params=_cparams("parallel"),
    )(*args)


def _rot_cols(w):
    half = w.shape[-1] // 2
    return jnp.concatenate([-w[..., half:], w[..., :half]], axis=-1)


def _pad_cols(w, width):
    return jnp.pad(w, [(0, 0)] * (w.ndim - 1) + [(0, width - w.shape[-1])])


def _prep_weights(l, P):
    w = {}
    win = P["w_mix_in"][l]
    o = _RW0 + RWKV_COLS
    kr = win[:, o + Q_RANK + KV_RANK:o + MLA_COLS]
    w["win"] = jnp.concatenate(
        [win[:, :o + Q_RANK + KV_RANK], win[:, o + MLA_COLS:], _pad_cols(kr, LANES),
         _pad_cols(_rot_cols(kr), LANES)], axis=1).astype(BF16)
    w["qn"] = P["mla_q_norm"][l].reshape(1, Q_RANK)
    w["kvn"] = P["mla_kv_norm"][l].reshape(1, KV_RANK)
    wuq = P["mla_w_uq"][l].reshape(Q_RANK, B_HEADS, NOPE_DIM + ROPE_DIM)
    nope, rope = wuq[..., :NOPE_DIM], wuq[..., NOPE_DIM:]
    z_n, z_r = jnp.zeros_like(nope), jnp.zeros_like(rope)
    w["wuq_p"] = jnp.stack([
        jnp.concatenate([nope, rope, z_r], -1).reshape(Q_RANK, B_HEADS * LANES),
        jnp.concatenate([z_n, _rot_cols(rope), z_r], -1).reshape(Q_RANK, B_HEADS * LANES)]).astype(BF16)
    w["wuq_s"] = jnp.concatenate(
        [nope.reshape(Q_RANK, -1), rope.reshape(Q_RANK, -1), _rot_cols(rope).reshape(Q_RANK, -1)],
        axis=1).astype(BF16)
    wuk = P["mla_w_uk"][l].reshape(KV_RANK, B_HEADS, NOPE_DIM)
    w["wuk_p"] = jnp.concatenate([wuk, jnp.zeros_like(wuk)], -1).reshape(KV_RANK, B_HEADS * LANES).astype(BF16)
    wukt = jnp.transpose(wuk, (1, 2, 0))
    zt = jnp.zeros_like(wukt)
    even = jnp.concatenate([wukt, zt], axis=1)
    odd = jnp.concatenate([zt, wukt], axis=1)
    w["wukt"] = jnp.where((jnp.arange(B_HEADS) % 2 == 0)[:, None, None], even, odd).astype(BF16)
    w["wuv"] = P["mla_w_uv"][l].astype(BF16)
    wuv_h = P["mla_w_uv"][l].reshape(KV_RANK, B_HEADS, V_DIM)
    eye = jnp.eye(B_HEADS, dtype=F32)
    w["wuv_bd"] = jnp.einsum("chd,hg->hcgd", wuv_h, eye).reshape(B_HEADS * KV_RANK, B_WIDTH).astype(BF16)
    w["mu"] = P["rwkv_mu"][l].reshape(1, RWKV_COLS)
    rvec = jnp.stack([P["rwkv_w0"][l], P["rwkv_a0"][l], P["rwkv_k_k"][l], P["rwkv_k_a"][l],
                      P["rwkv_r_k"][l].reshape(A_WIDTH)])
    w["rvec"] = jnp.pad(rvec, ((0, 3), (0, 0)))
    zl = jnp.zeros((W_LORA, A_WIDTH), F32)
    w["wup"] = jnp.concatenate([P["rwkv_w_up"][l], zl], 0).astype(BF16)
    w["aup"] = jnp.concatenate([zl, P["rwkv_a_up"][l]], 0).astype(BF16)
    w["gup"] = P["rwkv_g_up"][l].astype(BF16)
    w["ln"] = jnp.pad(jnp.stack([P["rwkv_ln_w"][l], P["rwkv_ln_b"][l]]), ((0, 6), (0, 0)))
    for name, key in (("wba", "w_branch_a"), ("wbb", "w_branch_b"), ("wbc", "w_branch_c"),
                      ("wo", "w_mix_out"), ("f1i", "ffn1_w_in"), ("f1o", "ffn1_w_out"),
                      ("f2i", "ffn2_w_in"), ("f2o", "ffn2_w_out")):
        w[name] = P[key][l].astype(BF16)
    return w


def _rope_tables(pos):
    half = ROPE_DIM // 2
    inv = ROPE_BASE ** (-jnp.arange(half, dtype=F32) / half)
    ang = pos.astype(F32)[:, None] * inv[None, :]
    c, s = jnp.cos(ang), jnp.sin(ang)
    return jnp.concatenate([c, c], 1), jnp.concatenate([s, s], 1)


def _tab_prompt(seq):
    c, s = _rope_tables(jnp.arange(seq, dtype=jnp.int32))
    one = jnp.ones((seq, NOPE_DIM), F32)
    zn = jnp.zeros((seq, NOPE_DIM), F32)
    zr = jnp.zeros((seq, LANES - NOPE_DIM - ROPE_DIM), F32)
    return jnp.concatenate([one, c, zr, zn, s, zr, _pad_cols(c, LANES), _pad_cols(s, LANES)], axis=1)


def _tab_sample(n, dec_seq, past_len):
    pos = past_len + (jnp.arange(n, dtype=jnp.int32) % dec_seq)
    c, s = _rope_tables(pos)
    return jnp.concatenate([jnp.tile(c, (1, B_HEADS)), jnp.tile(s, (1, B_HEADS)),
                            _pad_cols(c, LANES), _pad_cols(s, LANES)], axis=1)


def _pad_state(s0):
    z = jnp.zeros_like(s0)
    even = jnp.concatenate([s0, z], -1)
    odd = jnp.concatenate([z, s0], -1)
    return jnp.where((jnp.arange(A_HEADS) % 2 == 0)[None, :, None, None], even, odd)


def _unpad_state(sp):
    even, odd = sp[..., :A_HEAD_DIM], sp[..., A_HEAD_DIM:]
    return jnp.where((jnp.arange(A_HEADS) % 2 == 0)[None, :, None, None], even, odd)


def _layer_prompt(x, w, tab, batch, seq, final_g):
    x = _ffn(x, w["n1"], w["f1i"], w["f1o"])
    sg, pa, qf, kcat, vv, lat, krope, sbk, sbv, sbqkv = _proj_prompt(x, w["nm"], w, tab, seq)
    sin, pin, vt = _rwkv_prep_prompt(pa, w, batch, seq)
    sa, yp, sfin = _rwkv_scan(sin, vt)
    ya = _rwkv_post(sa.reshape(batch * seq, A_WIDTH), yp.reshape(batch * seq, A_WIDTH),
                    pin.reshape(batch * seq, 5 * A_WIDTH), w["ln"])
    sfin = jnp.transpose(sfin.reshape(batch, A_HEADS // 2, A_HEAD_DIM, 2, A_HEAD_DIM),
                         (0, 1, 3, 2, 4)).reshape(batch, A_HEADS, A_HEAD_DIM, A_HEAD_DIM)
    yb = _mla_prompt(qf, kcat, vv, batch, seq).reshape(batch * seq, B_WIDTH)
    yc = _sb_prompt(sbqkv, batch, seq).reshape(batch * seq, C_WIDTH)
    x = _merge(x, sg, ya, yb, yc, w, absorbed=False)
    x = _ffn(x, w["n2"], w["f2i"], w["f2o"], final_g=final_g)
    shift = pa.reshape(batch, seq, RWKV_COLS)[:, -1]
    ent = (lat.reshape(batch, seq, KV_RANK), krope.reshape(batch, seq, ROPE_DIM),
           sbk.reshape(batch, seq, C_HEADS, C_HEAD_DIM), sbv.reshape(batch, seq, C_HEADS, C_HEAD_DIM),
           sfin, shift)
    return x, ent


def _layer_sample(x, w, tab, l, page_table, caches, state, shift_prev, final_g):
    n = x.shape[0]
    x = _ffn(x, w["n1"], w["f1i"], w["f1o"])
    sg, pa, qlat, qrope, lat, krope, sbk, sbv, sbq = _proj_sample(x, w["nm"], w, tab)
    sin, pin = _rwkv_prep_sample(pa, shift_prev, w)
    sa, yp, sfin = _rwkv_step(sin, pin[:, :A_WIDTH], _pad_state(state))
    ya = _rwkv_post(sa, yp, pin, w["ln"])
    olat, osb = _decode_attn(l, page_table, qlat, qrope, sbq, lat, krope, *caches)
    x = _merge(x, sg, ya, olat, osb, w, absorbed=True)
    x = _ffn(x, w["n2"], w["f2i"], w["f2o"], final_g=final_g)
    ent = (lat.reshape(n, 1, KV_RANK), krope.reshape(n, 1, ROPE_DIM),
           sbk.reshape(n, 1, C_HEADS, C_HEAD_DIM), sbv.reshape(n, 1, C_HEADS, C_HEAD_DIM),
           _unpad_state(sfin), pa)
    return x, ent


def kernel(x_prompt, x_sample, cache_mla_latent, cache_mla_krope, cache_sb_k, cache_sb_v,
           state_rwkv, state_rwkv_shift, page_table,
           norm_ffn1, ffn1_w_in, ffn1_w_out, norm_mix, w_mix_in,
           rwkv_mu, rwkv_w0, rwkv_w_up, rwkv_a0, rwkv_a_up, rwkv_g_up, rwkv_k_k, rwkv_k_a,
           rwkv_r_k, rwkv_ln_w, rwkv_ln_b,
           mla_q_norm, mla_w_uq, mla_kv_norm, mla_w_uk, mla_w_uv,
           w_branch_a, w_branch_b, w_branch_c, w_mix_out,
           norm_ffn2, ffn2_w_in, ffn2_w_out, final_norm):
    P = dict(ffn1_w_in=ffn1_w_in, ffn1_w_out=ffn1_w_out, w_mix_in=w_mix_in, rwkv_mu=rwkv_mu,
             rwkv_w0=rwkv_w0, rwkv_w_up=rwkv_w_up, rwkv_a0=rwkv_a0, rwkv_a_up=rwkv_a_up,
             rwkv_g_up=rwkv_g_up, rwkv_k_k=rwkv_k_k, rwkv_k_a=rwkv_k_a, rwkv_r_k=rwkv_r_k,
             rwkv_ln_w=rwkv_ln_w, rwkv_ln_b=rwkv_ln_b, mla_q_norm=mla_q_norm, mla_w_uq=mla_w_uq,
             mla_kv_norm=mla_kv_norm, mla_w_uk=mla_w_uk, mla_w_uv=mla_w_uv, w_branch_a=w_branch_a,
             w_branch_b=w_branch_b, w_branch_c=w_branch_c, w_mix_out=w_mix_out,
             ffn2_w_in=ffn2_w_in, ffn2_w_out=ffn2_w_out)
    depth = norm_ffn1.shape[0]
    bp, tp, d = x_prompt.shape
    n_dec, dec_seq, _ = x_sample.shape
    assert dec_seq == 1, "the sample group carries one new token per sequence"
    past_len = page_table.shape[1] * PAGE_SIZE
    tab_p = _tab_prompt(tp)
    tab_s = _tab_sample(n_dec * dec_seq, dec_seq, past_len)
    xp = x_prompt.reshape(bp * tp, d)
    xs = x_sample.reshape(n_dec * dec_seq, d)
    caches = (cache_mla_latent, cache_mla_krope, cache_sb_k, cache_sb_v)
    new_p, new_s = [], []
    for l in range(depth):
        w = _prep_weights(l, P)
        w["n1"], w["nm"], w["n2"] = norm_ffn1[l], norm_mix[l], norm_ffn2[l]
        final_g = final_norm if l == depth - 1 else None
        xp, ent_p = _layer_prompt(xp, w, tab_p, bp, tp, final_g)
        xs, ent_s = _layer_sample(xs, w, tab_s, l, page_table, caches, state_rwkv[l],
                                  state_rwkv_shift[l], final_g)
        new_p.append(ent_p)
        new_s.append(ent_s)
    outs_p = [jnp.stack(t) for t in zip(*new_p)]
    outs_s = [jnp.stack(t) for t in zip(*new_s)]
    return (xp.reshape(bp, tp, d), xs.reshape(n_dec, dec_seq, d), *outs_p, *outs_s)
```

```python
import functools
import math

import jax
import jax.numpy as jnp
from jax import lax
from jax.experimental import pallas as pl
from jax.experimental.pallas import tpu as pltpu

F32 = jnp.float32
BF16 = jnp.bfloat16

D_MODEL = 1024
PAGE_SIZE = 128
A_HEADS, A_HEAD_DIM = 4, 64
A_WIDTH = A_HEADS * A_HEAD_DIM
W_LORA, A_LORA, G_LORA = 64, 64, 128
RWKV_COLS = 3 * A_WIDTH + W_LORA + A_LORA + G_LORA
GN_EPS = 64e-5
B_HEADS, Q_RANK, KV_RANK, NOPE_DIM, ROPE_DIM, V_DIM = 8, 256, 128, 64, 32, 64
B_WIDTH = B_HEADS * V_DIM
MLA_COLS = Q_RANK + KV_RANK + ROPE_DIM
ROPE_BASE = 10000.0
C_HEADS, C_HEAD_DIM = 4, 64
C_WIDTH = C_HEADS * C_HEAD_DIM
SB_COLS = 3 * C_WIDTH
FFN_DIM = 2816
EPS = 1e-6

LANES = 128
VMEM_LIMIT = 56 * 1024 * 1024

_G0 = 0
_RW0 = 3 * D_MODEL
_MQ0 = _RW0 + RWKV_COLS
_SB0 = _MQ0 + Q_RANK + KV_RANK
_KRA = _SB0 + SB_COLS
_KRB = _KRA + LANES
_WIN_COLS = _KRB + LANES

_NEG = -1e30


def _cparams(*sem):
    return pltpu.CompilerParams(dimension_semantics=sem, vmem_limit_bytes=VMEM_LIMIT)


def _const_spec(shape):
    nd = len(shape)
    return pl.BlockSpec(shape, lambda *_: (0,) * nd, pipeline_mode=pl.Buffered(1))


def _dot(a, b):
    return jnp.dot(a, b, preferred_element_type=F32)


def _dot_nt(a, b):
    return lax.dot_general(a, b, (((1,), (1,)), ((), ())), preferred_element_type=F32)


def _rms(x, g):
    return x * lax.rsqrt(jnp.mean(x * x, axis=-1, keepdims=True) + EPS) * g


def _softplus_parts(z):
    sp = jnp.log1p(jnp.exp(-jnp.abs(z)))
    return jnp.maximum(z, 0.0) + sp, jnp.minimum(z, 0.0) - sp


def _split3(x):
    x1 = x.astype(BF16)
    r1 = x - x1.astype(F32)
    x2 = r1.astype(BF16)
    x3 = (r1 - x2.astype(F32)).astype(BF16)
    return x1, x2, x3


def _exact_dot01(x, m):
    x1, x2, x3 = _split3(x)
    return _dot(x1, m) + _dot(x2, m) + _dot(x3, m)


def _head_ones(width, head):
    r = lax.broadcasted_iota(jnp.int32, (width, width), 0) // head
    c = lax.broadcasted_iota(jnp.int32, (width, width), 1) // head
    return jnp.where(r == c, 1.0, 0.0).astype(BF16)


_FFN_CHUNKS = ((0, FFN_DIM // 2), (FFN_DIM // 2, FFN_DIM))


def _ffn_apply(x, g_ref, win_ref, wout_ref, gf_ref):
    hb = _rms(x, g_ref[...]).astype(BF16)
    acc = None
    for a, b in _FFN_CHUNKS:
        gt = _dot(hb, win_ref[:, a:b])
        up = _dot(hb, win_ref[:, FFN_DIM + a:FFN_DIM + b])
        act = (gt * jax.nn.sigmoid(gt) * up).astype(BF16)
        part = _dot(act, wout_ref[a:b, :])
        acc = part if acc is None else acc + part
    y = x + 0.5 * acc
    return y if gf_ref is None else _rms(y, gf_ref[...])


def _ffn_kernel(x_ref, g_ref, win_ref, wout_ref, o_ref):
    o_ref[...] = _ffn_apply(x_ref[...], g_ref, win_ref, wout_ref, None)


def _ffn(x, g, win_b, wout_b, tm=512):
    n, d = x.shape
    tm = min(tm, n)
    return pl.pallas_call(
        _ffn_kernel,
        grid=(n // tm,),
        in_specs=[pl.BlockSpec((tm, d), lambda i: (i, 0)), _const_spec((1, d)),
                  _const_spec(win_b.shape), _const_spec(wout_b.shape)],
        out_specs=pl.BlockSpec((tm, d), lambda i: (i, 0)),
        out_shape=jax.ShapeDtypeStruct((n, d), F32),
        compiler_params=_cparams("parallel"),
    )(x, g.reshape(1, d), win_b, wout_b)


def _proj_common(x_ref, g_ref, win_ref, sg_ref, pa_ref):
    hb = _rms(x_ref[...], g_ref[...]).astype(BF16)
    sg_ref[...] = jax.nn.sigmoid(_dot(hb, win_ref[:, _G0:_RW0]))
    pa_ref[...] = _dot(hb, win_ref[:, _RW0:_MQ0])
    cm = _dot(hb, win_ref[:, _MQ0:_SB0])
    sb = _dot(hb, win_ref[:, _SB0:_KRA])
    kra = _dot(hb, win_ref[:, _KRA:_KRB])
    krb = _dot(hb, win_ref[:, _KRB:_WIN_COLS])
    return cm[:, :Q_RANK], cm[:, Q_RANK:], sb, kra, krb


def _proj_prompt_kernel(x_ref, g_ref, win_ref, tab_ref, qn_ref, kvn_ref, wuq_ref, wuk_ref, wuv_ref,
                        sg_ref, pa_ref, qf_ref, kcat_ref, vv_ref, lat_ref, krope_ref,
                        sbk_ref, sbv_ref, sbqkv_ref):
    cq, ckv, sb, kra, krb = _proj_common(x_ref, g_ref, win_ref, sg_ref, pa_ref)
    sbk_ref[...] = sb[:, C_WIDTH:2 * C_WIDTH]
    sbv_ref[...] = sb[:, 2 * C_WIDTH:]
    sbqkv_ref[...] = sb.astype(BF16)
    tab = tab_ref[...]
    kr = kra * tab[:, 2 * LANES:3 * LANES] + krb * tab[:, 3 * LANES:]
    krope_ref[...] = kr[:, :ROPE_DIM]
    lat = _rms(ckv, kvn_ref[...])
    lat_ref[...] = lat
    latb = lat.astype(BF16)
    hq = _rms(cq, qn_ref[...]).astype(BF16)
    q1 = _dot(hq, wuq_ref[0])
    q2 = _dot(hq, wuq_ref[1])
    kn = _dot(latb, wuk_ref[...])
    kadd = pltpu.roll(kr, NOPE_DIM, 1)
    cq_t, sq_t = tab[:, :LANES], tab[:, LANES:2 * LANES]
    for h in range(B_HEADS):
        sl = slice(h * LANES, (h + 1) * LANES)
        qf_ref[:, sl] = (q1[:, sl] * cq_t + q2[:, sl] * sq_t).astype(BF16)
        kcat_ref[:, sl] = (kn[:, sl] + kadd).astype(BF16)
    vv_ref[...] = _dot(latb, wuv_ref[...]).astype(BF16)


def _proj_sample_kernel(x_ref, g_ref, win_ref, tab_ref, qn_ref, kvn_ref, wuq_ref, wukt_ref,
                        sg_ref, pa_ref, qlat_ref, qrope_ref, lat_ref, krope_ref,
                        sbk_ref, sbv_ref, sbq_ref):
    cq, ckv, sb, kra, krb = _proj_common(x_ref, g_ref, win_ref, sg_ref, pa_ref)
    sbq_ref[...] = sb[:, :C_WIDTH].astype(BF16)
    sbk_ref[...] = sb[:, C_WIDTH:2 * C_WIDTH]
    sbv_ref[...] = sb[:, 2 * C_WIDTH:]
    tab = tab_ref[...]
    rw = B_HEADS * ROPE_DIM
    kr = kra * tab[:, 2 * rw:2 * rw + LANES] + krb * tab[:, 2 * rw + LANES:]
    krope_ref[...] = kr[:, :ROPE_DIM]
    lat_ref[...] = _rms(ckv, kvn_ref[...])
    hq = _rms(cq, qn_ref[...]).astype(BF16)
    qs = _dot(hq, wuq_ref[...])
    nw = B_HEADS * NOPE_DIM
    qn = qs[:, :nw].astype(BF16)
    qrope_ref[...] = (qs[:, nw:nw + rw] * tab[:, :rw] + qs[:, nw + rw:] * tab[:, rw:2 * rw]).astype(BF16)
    for h in range(B_HEADS):
        p = h // 2
        qlat_ref[:, h * LANES:(h + 1) * LANES] = _dot(
            qn[:, p * LANES:(p + 1) * LANES], wukt_ref[h]).astype(BF16)


def _proj_prompt(x, g, w, tab, seq, tm=256):
    n, d = x.shape
    nt = seq // tm
    row = lambda c: pl.BlockSpec((tm, c), lambda i: (i, 0))
    outs = [(3 * D_MODEL, F32), (RWKV_COLS, F32), (B_HEADS * LANES, BF16), (B_HEADS * LANES, BF16),
            (B_WIDTH, BF16), (KV_RANK, F32), (ROPE_DIM, F32), (C_WIDTH, F32), (C_WIDTH, F32),
            (SB_COLS, BF16)]
    return pl.pallas_call(
        _proj_prompt_kernel,
        grid=(n // tm,),
        in_specs=[row(d), _const_spec((1, d)), _const_spec(w["win"].shape),
                  pl.BlockSpec((tm, 4 * LANES), lambda i: (i % nt, 0)),
                  _const_spec((1, Q_RANK)), _const_spec((1, KV_RANK)),
                  _const_spec(w["wuq_p"].shape), _const_spec(w["wuk_p"].shape),
                  _const_spec(w["wuv"].shape)],
        out_specs=[row(c) for c, _ in outs],
        out_shape=[jax.ShapeDtypeStruct((n, c), dt) for c, dt in outs],
        compiler_params=_cparams("parallel"),
    )(x, g.reshape(1, d), w["win"], tab, w["qn"], w["kvn"], w["wuq_p"], w["wuk_p"], w["wuv"])


def _proj_sample(x, g, w, tab):
    n, d = x.shape
    tm = n
    row = lambda c: pl.BlockSpec((tm, c), lambda i: (i, 0))
    outs = [(3 * D_MODEL, F32), (RWKV_COLS, F32), (B_HEADS * LANES, BF16), (B_HEADS * ROPE_DIM, BF16),
            (KV_RANK, F32), (ROPE_DIM, F32), (C_WIDTH, F32), (C_WIDTH, F32), (C_WIDTH, BF16)]
    return pl.pallas_call(
        _proj_sample_kernel,
        grid=(n // tm,),
        in_specs=[row(d), _const_spec((1, d)), _const_spec(w["win"].shape), row(tab.shape[1]),
                  _const_spec((1, Q_RANK)), _const_spec((1, KV_RANK)),
                  _const_spec(w["wuq_s"].shape), _const_spec(w["wukt"].shape)],
        out_specs=[row(c) for c, _ in outs],
        out_shape=[jax.ShapeDtypeStruct((n, c), dt) for c, dt in outs],
        compiler_params=_cparams("parallel"),
    )(x, g.reshape(1, d), w["win"], tab, w["qn"], w["kvn"], w["wuq_s"], w["wukt"])


def _rwkv_prep_kernel(*refs, prompt):
    if prompt:
        (pa_ref, prev_ref, mu_ref, vec_ref, wup_ref, aup_ref, gup_ref,
         sin_ref, pin_ref, vt_ref) = refs
        p = pa_ref[...]
        tt = p.shape[0]
        first = pl.program_id(1) == 0
        prev_row = jnp.where(first, 0.0, prev_ref[7:8, :])
        rows = lax.broadcasted_iota(jnp.int32, p.shape, 0)
        p_prev = jnp.where(rows == 0, prev_row, pltpu.roll(p, 1, 0))
    else:
        (pa_ref, prev_ref, mu_ref, vec_ref, wup_ref, aup_ref, gup_ref,
         sin_ref, pin_ref) = refs
        p = pa_ref[...]
        p_prev = prev_ref[...]
    xm = p + (p_prev - p) * mu_ref[...]
    aw = A_WIDTH
    r, k, v = xm[:, :aw], xm[:, aw:2 * aw], xm[:, 2 * aw:3 * aw]
    wa = xm[:, 3 * aw:3 * aw + LANES]
    gd = xm[:, 3 * aw + LANES:]
    vec = vec_ref[...]
    w0, a0, k_k, k_a, r_k = (vec[i:i + 1, :] for i in range(5))
    lw = _dot(jnp.tanh(wa).astype(BF16), wup_ref[...])
    la = _dot(wa.astype(BF16), aup_ref[...])
    sp, _ = _softplus_parts(-(w0 + lw))
    decay = jnp.exp(-jnp.exp(-sp - 0.5))
    a = jax.nn.sigmoid(a0 + la)
    g = _dot(jax.nn.sigmoid(gd).astype(BF16), gup_ref[...])
    ones = _head_ones(aw, A_HEAD_DIM)
    kk = k * k_k
    kk = kk / jnp.maximum(jnp.sqrt(_exact_dot01(kk * kk, ones)), 1e-12)
    km = k * (1.0 + (a - 1.0) * k_a)
    b = kk * a
    sin_ref[:, 0 * aw:1 * aw] = decay
    sin_ref[:, 1 * aw:2 * aw] = -kk
    sin_ref[:, 2 * aw:3 * aw] = b
    sin_ref[:, 3 * aw:4 * aw] = km
    sin_ref[:, 4 * aw:5 * aw] = decay * r
    pin_ref[:, 0 * aw:1 * aw] = v
    pin_ref[:, 1 * aw:2 * aw] = _exact_dot01(b * r, ones)
    pin_ref[:, 2 * aw:3 * aw] = _exact_dot01(km * r, ones)
    pin_ref[:, 3 * aw:4 * aw] = _exact_dot01(r * km * r_k, ones)
    pin_ref[:, 4 * aw:5 * aw] = g
    if prompt:
        for p in range(A_HEADS // 2):
            for blk in range(tt // BLK):
                vt_ref[p, blk] = _pair_transpose(v[blk * BLK:(blk + 1) * BLK, p * LANES:(p + 1) * LANES])


def _rwkv_prep_prompt(pa, w, batch, seq, tt=256):
    n = pa.shape[0]
    nt = seq // tt
    pa3 = pa.reshape(batch, seq, RWKV_COLS)
    aw5 = 5 * A_WIDTH
    sin, pin, vt = pl.pallas_call(
        functools.partial(_rwkv_prep_kernel, prompt=True),
        grid=(batch, nt),
        in_specs=[pl.BlockSpec((None, tt, RWKV_COLS), lambda b, i: (b, i, 0)),
                  pl.BlockSpec((None, 8, RWKV_COLS), lambda b, i: (b, jnp.maximum(i * (tt // 8) - 1, 0), 0)),
                  _const_spec((1, RWKV_COLS)), _const_spec((8, A_WIDTH)),
                  _const_spec((LANES, A_WIDTH)), _const_spec((LANES, A_WIDTH)),
                  _const_spec((G_LORA, A_WIDTH))],
        out_specs=[pl.BlockSpec((None, tt, aw5), lambda b, i: (b, i, 0)),
                   pl.BlockSpec((None, tt, aw5), lambda b, i: (b, i, 0)),
                   pl.BlockSpec((None, A_HEADS // 2, tt // BLK, A_HEAD_DIM, LANES),
                                lambda b, i: (b, 0, i, 0, 0))],
        out_shape=[jax.ShapeDtypeStruct((batch, seq, aw5), F32),
                   jax.ShapeDtypeStruct((batch, seq, aw5), F32),
                   jax.ShapeDtypeStruct((batch, A_HEADS // 2, seq // BLK, A_HEAD_DIM, LANES), F32)],
        compiler_params=_cparams("parallel", "parallel"),
    )(pa3, pa3, w["mu"], w["rvec"], w["wup"], w["aup"], w["gup"])
    return sin, pin, vt


def _rwkv_prep_sample(pa, shift, w):
    n = pa.shape[0]
    aw5 = 5 * A_WIDTH
    full = lambda c: pl.BlockSpec((n, c), lambda i: (0, 0))
    sin, pin = pl.pallas_call(
        functools.partial(_rwkv_prep_kernel, prompt=False),
        grid=(1,),
        in_specs=[full(RWKV_COLS), full(RWKV_COLS),
                  _const_spec((1, RWKV_COLS)), _const_spec((8, A_WIDTH)),
                  _const_spec((LANES, A_WIDTH)), _const_spec((LANES, A_WIDTH)),
                  _const_spec((G_LORA, A_WIDTH))],
        out_specs=[full(aw5), full(aw5)],
        out_shape=[jax.ShapeDtypeStruct((n, aw5), F32), jax.ShapeDtypeStruct((n, aw5), F32)],
        compiler_params=_cparams("arbitrary"),
    )(pa, shift, w["mu"], w["rvec"], w["wup"], w["aup"], w["gup"])
    return sin, pin


BLK = 64


def _pair_transpose(x):
    z = jnp.concatenate([x, pltpu.roll(x, A_HEAD_DIM, 1)], axis=0)
    return z.T[:A_HEAD_DIM, :]


def _rwkv_scan_kernel(sin_ref, vt_ref, sa_ref, yp_ref, sfin_ref, s_ref, csa_ref, cyp_ref, *, bb, grp):
    tb = pl.program_id(1)

    @pl.when(tb == 0)
    def _():
        s_ref[...] = jnp.zeros_like(s_ref)

    aw = A_WIDTH
    ones = _head_ones(2 * LANES, A_HEAD_DIM)
    lane = lax.broadcasted_iota(jnp.int32, (A_HEAD_DIM, LANES), 1)
    half = (lane // A_HEAD_DIM) * A_HEAD_DIM
    step_lane = lane - half
    pairs = [(b, p) for b in range(bb) for p in range(A_HEADS // 2)]
    sub = 8

    def body(tg, carry):
        t0 = pl.multiple_of(tg * sub, sub)
        tiles = {}
        for b, p in pairs:
            tiles[b, p] = [sin_ref[b, pl.ds(t0, sub), q * aw + p * LANES:q * aw + (p + 1) * LANES]
                           for q in range(5)]
        for u in range(sub):
            sel = step_lane == t0 + u
            idx = half + (t0 + u)
            for g0 in range(0, len(pairs), grp):
                group = pairs[g0:g0 + grp]
                his, los = [], []
                for bp in group:
                    s = s_ref[bp]
                    w, kkn, bv, km, wr = (x[u:u + 1, :] for x in tiles[bp])
                    f = jnp.concatenate([s * kkn, s * wr], axis=1)
                    hi = f.astype(BF16)
                    his.append(hi)
                    los.append((f - hi.astype(F32)).astype(BF16))
                red = _dot(jnp.concatenate(his + los, axis=0), ones)
                for k, bp in enumerate(group):
                    w, kkn, bv, km, wr = (x[u:u + 1, :] for x in tiles[bp])
                    r = (red[k * A_HEAD_DIM:(k + 1) * A_HEAD_DIM]
                         + red[(grp + k) * A_HEAD_DIM:(grp + k + 1) * A_HEAD_DIM])
                    sa, yp = r[:, :LANES], r[:, LANES:]
                    vc = jnp.take_along_axis(vt_ref[bp], idx, axis=1)
                    s_ref[bp] = s_ref[bp] * w + sa * bv + vc * km
                    pltpu.store(csa_ref.at[bp], sa, mask=sel)
                    pltpu.store(cyp_ref.at[bp], yp, mask=sel)
        return carry

    lax.fori_loop(0, BLK // sub, body, 0)
    for b, p in pairs:
        sa_ref[b, :, p * LANES:(p + 1) * LANES] = _pair_transpose(csa_ref[b, p])
        yp_ref[b, :, p * LANES:(p + 1) * LANES] = _pair_transpose(cyp_ref[b, p])

    @pl.when(tb == pl.num_programs(1) - 1)
    def _():
        sfin_ref[...] = s_ref[...]


def _rwkv_scan(sin, vt2, bb=8, grp=4):
    batch, seq, aw5 = sin.shape
    assert batch % bb == 0 and seq % BLK == 0 and (bb * (A_HEADS // 2)) % grp == 0, (batch, seq, bb, grp)
    npair = A_HEADS // 2
    rows = pl.BlockSpec((bb, BLK, A_WIDTH), lambda g, i: (g, i, 0))
    sblk = pl.BlockSpec((bb, npair, A_HEAD_DIM, LANES), lambda g, i: (g, 0, 0, 0))
    tile = pltpu.VMEM((bb, npair, A_HEAD_DIM, LANES), F32)
    return pl.pallas_call(
        functools.partial(_rwkv_scan_kernel, bb=bb, grp=grp),
        grid=(batch // bb, seq // BLK),
        in_specs=[pl.BlockSpec((bb, BLK, aw5), lambda g, i: (g, i, 0)),
                  pl.BlockSpec((bb, npair, None, A_HEAD_DIM, LANES), lambda g, i: (g, 0, i, 0, 0))],
        out_specs=[rows, rows, sblk],
        out_shape=[jax.ShapeDtypeStruct((batch, seq, A_WIDTH), F32),
                   jax.ShapeDtypeStruct((batch, seq, A_WIDTH), F32),
                   jax.ShapeDtypeStruct((batch, npair, A_HEAD_DIM, LANES), F32)],
        scratch_shapes=[tile, tile, tile],
        compiler_params=_cparams("parallel", "arbitrary"),
    )(sin, vt2)


def _rwkv_step_kernel(sin_ref, v_ref, s0_ref, sa_ref, yp_ref, sfin_ref, *, bb):
    aw = A_WIDTH
    lane_j = lax.broadcasted_iota(jnp.int32, (1, LANES), 1)
    for b in range(bb):
        for h in range(A_HEADS):
            p = h // 2
            own = (lane_j // A_HEAD_DIM) == (h % 2)
            rowv = lambda q: sin_ref[b, :, q * aw + p * LANES:q * aw + (p + 1) * LANES]
            s = s0_ref[b, h]
            sa = jnp.sum(s * rowv(1), axis=1, keepdims=True)
            yp = jnp.sum(s * rowv(4), axis=1, keepdims=True)
            sfin_ref[b, h] = (s * rowv(0) + sa * jnp.where(own, rowv(2), 0.0)
                              + v_ref[b, h] * jnp.where(own, rowv(3), 0.0))
            sa_ref[b, h] = sa
            yp_ref[b, h] = yp


def _rwkv_step(sin, v, s0p, bb=4):
    n, aw5 = sin.shape
    col = pl.BlockSpec((bb, A_HEADS, A_HEAD_DIM, 1), lambda g: (g, 0, 0, 0))
    sblk = pl.BlockSpec((bb, A_HEADS, A_HEAD_DIM, LANES), lambda g: (g, 0, 0, 0))
    col_shape = jax.ShapeDtypeStruct((n, A_HEADS, A_HEAD_DIM, 1), F32)
    sa, yp, sfin = pl.pallas_call(
        functools.partial(_rwkv_step_kernel, bb=bb),
        grid=(n // bb,),
        in_specs=[pl.BlockSpec((bb, 1, aw5), lambda g: (g, 0, 0)), col, sblk],
        out_specs=[col, col, sblk],
        out_shape=[col_shape, col_shape, jax.ShapeDtypeStruct((n, A_HEADS, A_HEAD_DIM, LANES), F32)],
        compiler_params=_cparams("parallel"),
    )(sin.reshape(n, 1, aw5), v.reshape(n, A_HEADS, A_HEAD_DIM, 1), s0p)
    return sa.reshape(n, A_WIDTH), yp.reshape(n, A_WIDTH), sfin


def _rwkv_post_kernel(sa_ref, yp_ref, pin_ref, ln_ref, ya_ref):
    aw = A_WIDTH
    sa, yp = sa_ref[...], yp_ref[...]
    pin = pin_ref[...]
    v, brb, krb, bc, g = (pin[:, i * aw:(i + 1) * aw] for i in range(5))
    y = yp + sa * brb + v * krb
    ones = _head_ones(aw, A_HEAD_DIM)
    inv = 1.0 / A_HEAD_DIM
    mean = _exact_dot01(y, ones) * inv
    dlt = y - mean
    var = _exact_dot01(dlt * dlt, ones) * inv
    yn = dlt * lax.rsqrt(var + GN_EPS) * ln_ref[0:1, :] + ln_ref[1:2, :]
    ya_ref[...] = ((yn + bc * v) * g).astype(BF16)


def _rwkv_post(sa, yp, pin, ln, tm=512):
    n = pin.shape[0]
    tm = min(tm, n)
    row = lambda c: pl.BlockSpec((tm, c), lambda i: (i, 0))
    return pl.pallas_call(
        _rwkv_post_kernel,
        grid=(n // tm,),
        in_specs=[row(A_WIDTH), row(A_WIDTH), row(5 * A_WIDTH), _const_spec((8, A_WIDTH))],
        out_specs=row(A_WIDTH),
        out_shape=jax.ShapeDtypeStruct((n, A_WIDTH), BF16),
        compiler_params=_cparams("parallel"),
    )(sa, yp, pin, ln)


def _mla_prompt_kernel(q_ref, k_ref, v_ref, o_ref, *, tq, scale, hpl):
    i = pl.program_id(1)
    row = lax.broadcasted_iota(jnp.int32, (tq, tq), 0)
    col = lax.broadcasted_iota(jnp.int32, (tq, tq), 1)
    causal = col <= row
    lane = lax.broadcasted_iota(jnp.int32, (tq, LANES), 1)
    c2 = scale * math.log2(math.e)
    for h0 in range(0, B_HEADS, hpl):
        heads = list(range(h0, h0 + hpl))
        qs = [q_ref[:, h * LANES:(h + 1) * LANES] for h in heads]

        def step(j, carry, diag, heads=heads, qs=qs):
            off = pl.multiple_of(j * tq, tq)
            out = []
            for n, h in enumerate(heads):
                m, l, acc = carry[n]
                s = _dot_nt(qs[n], k_ref[pl.ds(off, tq), h * LANES:(h + 1) * LANES])
                if diag:
                    s = jnp.where(causal, s, _NEG)
                m_new = jnp.maximum(m, jnp.max(s, axis=-1, keepdims=True))
                alpha = jnp.exp2((m - m_new) * c2)
                pr = jnp.exp2((s - m_new) * c2)
                l = alpha * l + jnp.sum(pr, axis=-1, keepdims=True)
                v = v_ref[pl.ds(off, tq), (h // 2) * LANES:(h // 2 + 1) * LANES]
                out.append((m_new, l, alpha * acc + _dot(pr.astype(BF16), v)))
            return tuple(out)

        init = tuple((jnp.full((tq, 1), _NEG, F32), jnp.zeros((tq, 1), F32), jnp.zeros((tq, LANES), F32))
                     for _ in heads)
        carry = lax.fori_loop(0, i, functools.partial(step, diag=False), init)
        res = [acc / l for _, l, acc in step(i, carry, True)]
        for n in range(0, hpl, 2):
            p = heads[n] // 2
            o_ref[:, p * LANES:(p + 1) * LANES] = jnp.where(lane < V_DIM, res[n], res[n + 1]).astype(BF16)


def _mla_prompt(qf, kcat, vv, batch, seq, tq=512, hpl=B_HEADS):
    assert seq % tq == 0, (seq, tq)
    hw = B_HEADS * LANES
    scale = 1.0 / math.sqrt(NOPE_DIM + ROPE_DIM)
    return pl.pallas_call(
        functools.partial(_mla_prompt_kernel, tq=tq, scale=scale, hpl=hpl),
        grid=(batch, seq // tq),
        in_specs=[pl.BlockSpec((None, tq, hw), lambda b, i: (b, i, 0)),
                  pl.BlockSpec((None, seq, hw), lambda b, i: (b, 0, 0)),
                  pl.BlockSpec((None, seq, B_WIDTH), lambda b, i: (b, 0, 0))],
        out_specs=pl.BlockSpec((None, tq, B_WIDTH), lambda b, i: (b, i, 0)),
        out_shape=jax.ShapeDtypeStruct((batch, seq, B_WIDTH), BF16),
        compiler_params=_cparams("parallel", "arbitrary"),
    )(qf.reshape(batch, seq, hw), kcat.reshape(batch, seq, hw), vv.reshape(batch, seq, B_WIDTH))


def _sb_weights(z, c, upper, valid):
    soft, logsig = _softplus_parts(z)
    if valid is not None:
        soft = jnp.where(valid, soft, 0.0)
    hi = soft.astype(BF16)
    lo = (soft - hi.astype(F32)).astype(BF16)
    later = _dot(hi, upper) + _dot(lo, upper)
    a = jnp.exp(logsig - later - c)
    if valid is not None:
        a = jnp.where(valid, a, 0.0)
    return a, c + jnp.sum(soft, axis=-1, keepdims=True)


def _upper_ones(tk):
    r = lax.broadcasted_iota(jnp.int32, (tk, tk), 0)
    c = lax.broadcasted_iota(jnp.int32, (tk, tk), 1)
    return jnp.where(r > c, 1.0, 0.0).astype(BF16)


def _sb_prompt_kernel(q_ref, k_ref, v_ref, o_ref, *, tq, scale):
    i = pl.program_id(1)
    row = lax.broadcasted_iota(jnp.int32, (tq, tq), 0)
    col = lax.broadcasted_iota(jnp.int32, (tq, tq), 1)
    strict = col < row
    upper = _upper_ones(tq)
    lane = lax.broadcasted_iota(jnp.int32, (tq, LANES), 1)
    qms = []
    for h in range(C_HEADS):
        qp = q_ref[:, (h // 2) * LANES:(h // 2 + 1) * LANES].astype(F32)
        qms.append(jnp.where((lane // C_HEAD_DIM) == (h % 2), qp, 0.0).astype(BF16))

    def step(j, carry, diag):
        off = pl.multiple_of(j * tq, tq)
        out = []
        for h in range(C_HEADS):
            c, acc = carry[h]
            psl = slice((h // 2) * LANES, (h // 2 + 1) * LANES)
            z = _dot_nt(qms[h], k_ref[pl.ds(off, tq), psl]) * scale
            a, c = _sb_weights(z, c, upper, strict if diag else None)
            out.append((c, acc + _dot(a.astype(BF16), v_ref[pl.ds(off, tq), psl])))
        return tuple(out)

    init = tuple((jnp.zeros((tq, 1), F32), jnp.zeros((tq, LANES), F32)) for _ in range(C_HEADS))
    carry = step(i, init, True)
    res = lax.fori_loop(0, i, lambda jj, cr: step(i - 1 - jj, cr, False), carry)
    for p in range(C_HEADS // 2):
        o_ref[:, p * LANES:(p + 1) * LANES] = jnp.where(
            lane < C_HEAD_DIM, res[2 * p][1], res[2 * p + 1][1]).astype(BF16)


def _sb_prompt(sbqkv, batch, seq, tq=512):
    assert seq % tq == 0, (seq, tq)
    x = sbqkv.reshape(batch, seq, SB_COLS)
    scale = C_HEAD_DIM ** -0.5
    return pl.pallas_call(
        functools.partial(_sb_prompt_kernel, tq=tq, scale=scale),
        grid=(batch, seq // tq),
        in_specs=[pl.BlockSpec((None, tq, C_WIDTH), lambda b, i: (b, i, 0)),
                  pl.BlockSpec((None, seq, C_WIDTH), lambda b, i: (b, 0, 1)),
                  pl.BlockSpec((None, seq, C_WIDTH), lambda b, i: (b, 0, 2))],
        out_specs=pl.BlockSpec((None, tq, C_WIDTH), lambda b, i: (b, i, 0)),
        out_shape=jax.ShapeDtypeStruct((batch, seq, C_WIDTH), BF16),
        compiler_params=_cparams("parallel", "arbitrary"),
    )(x, x, x)


def _decode_kernel(pt_ref, qlat_ref, qrope_ref, sbq_ref, latn_ref, krn_ref,
                   clat_hbm, ckrt_hbm, ckt_hbm, cvt_hbm, olat_ref, osb_ref,
                   lat_buf, krt_buf, kt_buf, vt_buf, s_scr, z_scr, sems, *, layer, n_pages, scale_b, scale_c):
    b = pl.program_id(0)
    srcs = (clat_hbm, ckrt_hbm, ckt_hbm, cvt_hbm)
    bufs = (lat_buf, krt_buf, kt_buf, vt_buf)
    unroll = math.gcd(16, n_pages)
    slot = b % 2

    def page_copy(a, j, seq, sl):
        return pltpu.make_async_copy(srcs[a].at[layer, pt_ref[seq, j]], bufs[a].at[sl, j], sems.at[sl, a])

    def for_pages(fn, init=0):
        def body(jj, carry):
            for u in range(unroll):
                carry = fn(jj * unroll + u, carry)
            return carry
        return lax.fori_loop(0, n_pages // unroll, body, init)

    def start_all(seq, sl):
        for arrs in ((0, 1), (2,), (3,)):
            def fn(j, c, arrs=arrs):
                for a in arrs:
                    page_copy(a, j, seq, sl).start()
                return c
            for_pages(fn)

    def wait(arrs):
        def fn(j, c):
            for a in arrs:
                page_copy(a, j, b, slot).wait()
            return c
        for_pages(fn)

    @pl.when(b == 0)
    def _():
        start_all(0, 0)

    @pl.when(b + 1 < pl.num_programs(0))
    def _():
        start_all(b + 1, 1 - slot)

    rows = 8
    lane_c = lax.broadcasted_iota(jnp.int32, (rows, C_WIDTH), 1)
    row_c = lax.broadcasted_iota(jnp.int32, (rows, C_WIDTH), 0)
    own = (lane_c // C_HEAD_DIM) == row_c
    ql, qr = qlat_ref[...], qrope_ref[...]
    qm = jnp.where(own, jnp.broadcast_to(sbq_ref[...].astype(F32), own.shape), 0.0).astype(BF16)
    latn = latn_ref[...].astype(BF16).astype(F32)
    krn = krn_ref[...].astype(BF16).astype(F32)
    s_self = (jnp.sum(ql.astype(F32) * latn, axis=-1, keepdims=True)
              + jnp.sum(qr.astype(F32) * krn, axis=-1, keepdims=True)) * scale_b

    wait((0, 1))

    def score_b(j, c):
        s_scr[j] = (_dot_nt(ql, lat_buf[slot, j].astype(BF16))
                    + _dot(qr, krt_buf[slot, j].astype(BF16))) * scale_b
        return c
    for_pages(score_b)
    s = s_scr[...]
    m = jnp.maximum(jnp.max(jnp.max(s, axis=0), axis=-1, keepdims=True), s_self)
    pr = jnp.exp(s - m)
    p_self = jnp.exp(s_self - m)
    l = jnp.sum(jnp.sum(pr, axis=0), axis=-1, keepdims=True) + p_self
    s_scr[...] = pr
    acc = for_pages(lambda j, acc: acc + _dot(s_scr[j].astype(BF16), lat_buf[slot, j].astype(BF16)),
                    jnp.zeros((rows, KV_RANK), F32))
    olat_ref[...] = (acc + p_self * latn) / l

    wait((2,))

    def score_c(j, c):
        z_scr[j] = _dot(qm, kt_buf[slot, j].astype(BF16)) * scale_c
        return c
    for_pages(score_c)
    soft, logsig = _softplus_parts(z_scr[...])
    flat = soft.reshape(n_pages * rows, PAGE_SIZE)
    hi = flat.astype(BF16)
    lo = (flat - hi.astype(F32)).astype(BF16)
    upper = _upper_ones(PAGE_SIZE)
    later = (_dot(hi, upper) + _dot(lo, upper)).reshape(soft.shape)
    tot = jnp.broadcast_to(jnp.sum(soft, axis=-1, keepdims=True), soft.shape)
    suf = tot
    k = 1
    while k < n_pages:
        suf = suf + jnp.concatenate([suf[k:], jnp.zeros((k,) + suf.shape[1:], F32)], axis=0)
        k *= 2
    z_scr[...] = jnp.exp(logsig - later - (suf - tot))
    wait((3,))
    accs = for_pages(lambda j, acc: acc + _dot_nt(z_scr[j].astype(BF16), vt_buf[slot, j].astype(BF16)),
                     jnp.zeros((rows, C_WIDTH), F32))
    osb_ref[...] = jnp.sum(jnp.where(own, accs, 0.0), axis=0, keepdims=True)


def _decode_attn(layer, page_table, qlat, qrope, sbq, lat_new, kr_new, c_lat, c_kr, c_k, c_v):
    n, n_pages = page_table.shape
    c_krt = jnp.swapaxes(c_kr, 2, 3)
    c_kt = jnp.transpose(c_k, (0, 1, 3, 4, 2)).reshape(c_k.shape[:2] + (C_WIDTH, PAGE_SIZE))
    c_vt = jnp.transpose(c_v, (0, 1, 3, 4, 2)).reshape(c_v.shape[:2] + (C_WIDTH, PAGE_SIZE))
    caches = (c_lat, c_krt, c_kt, c_vt)
    tok = lambda r, c: pl.BlockSpec((None, r, c), lambda b, pt: (b, 0, 0))
    grid_spec = pltpu.PrefetchScalarGridSpec(
        num_scalar_prefetch=1,
        grid=(n,),
        in_specs=[tok(B_HEADS, KV_RANK), tok(B_HEADS, ROPE_DIM), tok(1, C_WIDTH), tok(1, KV_RANK),
                  tok(1, ROPE_DIM)] + [pl.BlockSpec(memory_space=pl.ANY)] * 4,
        out_specs=[tok(B_HEADS, KV_RANK), tok(1, C_WIDTH)],
        scratch_shapes=[pltpu.VMEM((2, n_pages) + c.shape[2:], F32) for c in caches]
        + [pltpu.VMEM((n_pages, 8, PAGE_SIZE), F32), pltpu.VMEM((n_pages, 8, PAGE_SIZE), F32),
           pltpu.SemaphoreType.DMA((2, 4))],
    )
    olat, osb = pl.pallas_call(
        functools.partial(_decode_kernel, layer=layer, n_pages=n_pages,
                          scale_b=1.0 / math.sqrt(NOPE_DIM + ROPE_DIM), scale_c=C_HEAD_DIM ** -0.5),
        grid_spec=grid_spec,
        out_shape=[jax.ShapeDtypeStruct((n, B_HEADS, KV_RANK), F32),
                   jax.ShapeDtypeStruct((n, 1, C_WIDTH), F32)],
        compiler_params=_cparams("arbitrary"),
    )(page_table, qlat.reshape(n, B_HEADS, KV_RANK), qrope.reshape(n, B_HEADS, ROPE_DIM),
      sbq.reshape(n, 1, C_WIDTH), lat_new.reshape(n, 1, KV_RANK), kr_new.reshape(n, 1, ROPE_DIM), *caches)
    return olat.reshape(n, B_HEADS * KV_RANK), osb.reshape(n, C_WIDTH)


def _merge_kernel(*refs, absorbed, final):
    refs = list(refs)
    o_ref = refs.pop()
    gf_ref = refs.pop() if final else None
    x_ref, sg_ref, ya_ref, yb_ref, yc_ref = refs[:5]
    if absorbed:
        wuv_ref = refs[5]
        yb = _dot(yb_ref[...].astype(BF16), wuv_ref[...]).astype(BF16)
    else:
        yb = yb_ref[...]
    wba_ref, wbb_ref, wbc_ref, wo_ref, g2_ref, f2i_ref, f2o_ref = refs[-7:]
    d = D_MODEL
    merged = (sg_ref[:, :d] * _dot(ya_ref[...], wba_ref[...])
              + sg_ref[:, d:2 * d] * _dot(yb, wbb_ref[...])
              + sg_ref[:, 2 * d:] * _dot(yc_ref[...].astype(BF16), wbc_ref[...]))
    x = x_ref[...] + _dot(merged.astype(BF16), wo_ref[...])
    o_ref[...] = _ffn_apply(x, g2_ref, f2i_ref, f2o_ref, gf_ref)


def _merge(x, sg, ya, yb, yc, w, absorbed, final_g, tm=512):
    n, d = x.shape
    tm = min(tm, n)
    row = lambda c: pl.BlockSpec((tm, c), lambda i: (i, 0))
    in_specs = [row(d), row(3 * d), row(A_WIDTH), row(yb.shape[1]), row(C_WIDTH)]
    args = [x, sg, ya, yb, yc]
    if absorbed:
        in_specs.append(_const_spec(w["wuv_bd"].shape))
        args.append(w["wuv_bd"])
    for name in ("wba", "wbb", "wbc", "wo"):
        in_specs.append(_const_spec(w[name].shape))
        args.append(w[name])
    in_specs += [_const_spec((1, d)), _const_spec(w["f2i"].shape), _const_spec(w["f2o"].shape)]
    args += [w["n2"].reshape(1, d), w["f2i"], w["f2o"]]
    if final_g is not None:
        in_specs.append(_const_spec((1, d)))
        args.append(final_g.reshape(1, d))
    return pl.pallas_call(
        functools.partial(_merge_kernel, absorbed=absorbed, final=final_g is not None),
        grid=(n // tm,),
        in_specs=in_specs,
        out_specs=row(d),
        out_shape=jax.ShapeDtypeStruct((n, d), F32),
        compiler_params=_cparams("parallel"),
    )(*args)


def _rot_cols(w):
    half = w.shape[-1] // 2
    return jnp.concatenate([-w[..., half:], w[..., :half]], axis=-1)


def _pad_cols(w, width):
    return jnp.pad(w, [(0, 0)] * (w.ndim - 1) + [(0, width - w.shape[-1])])


def _prep_weights(l, P):
    w = {}
    win = P["w_mix_in"][l]
    o = _RW0 + RWKV_COLS
    kr = win[:, o + Q_RANK + KV_RANK:o + MLA_COLS]
    w["win"] = jnp.concatenate(
        [win[:, :o + Q_RANK + KV_RANK], win[:, o + MLA_COLS:], _pad_cols(kr, LANES),
         _pad_cols(_rot_cols(kr), LANES)], axis=1).astype(BF16)
    w["qn"] = P["mla_q_norm"][l].reshape(1, Q_RANK)
    w["kvn"] = P["mla_kv_norm"][l].reshape(1, KV_RANK)
    wuq = P["mla_w_uq"][l].reshape(Q_RANK, B_HEADS, NOPE_DIM + ROPE_DIM)
    nope, rope = wuq[..., :NOPE_DIM], wuq[..., NOPE_DIM:]
    z_n, z_r = jnp.zeros_like(nope), jnp.zeros_like(rope)
    w["wuq_p"] = jnp.stack([
        jnp.concatenate([nope, rope, z_r], -1).reshape(Q_RANK, B_HEADS * LANES),
        jnp.concatenate([z_n, _rot_cols(rope), z_r], -1).reshape(Q_RANK, B_HEADS * LANES)]).astype(BF16)
    w["wuq_s"] = jnp.concatenate(
        [nope.reshape(Q_RANK, -1), rope.reshape(Q_RANK, -1), _rot_cols(rope).reshape(Q_RANK, -1)],
        axis=1).astype(BF16)
    wuk = P["mla_w_uk"][l].reshape(KV_RANK, B_HEADS, NOPE_DIM)
    w["wuk_p"] = jnp.concatenate([wuk, jnp.zeros_like(wuk)], -1).reshape(KV_RANK, B_HEADS * LANES).astype(BF16)
    wukt = jnp.transpose(wuk, (1, 2, 0))
    zt = jnp.zeros_like(wukt)
    even = jnp.concatenate([wukt, zt], axis=1)
    odd = jnp.concatenate([zt, wukt], axis=1)
    w["wukt"] = jnp.where((jnp.arange(B_HEADS) % 2 == 0)[:, None, None], even, odd).astype(BF16)
    w["wuv"] = P["mla_w_uv"][l].astype(BF16)
    wuv_h = P["mla_w_uv"][l].reshape(KV_RANK, B_HEADS, V_DIM)
    eye = jnp.eye(B_HEADS, dtype=F32)
    w["wuv_bd"] = jnp.einsum("chd,hg->hcgd", wuv_h, eye).reshape(B_HEADS * KV_RANK, B_WIDTH).astype(BF16)
    w["mu"] = P["rwkv_mu"][l].reshape(1, RWKV_COLS)
    rvec = jnp.stack([P["rwkv_w0"][l], P["rwkv_a0"][l], P["rwkv_k_k"][l], P["rwkv_k_a"][l],
                      P["rwkv_r_k"][l].reshape(A_WIDTH)])
    w["rvec"] = jnp.pad(rvec, ((0, 3), (0, 0)))
    zl = jnp.zeros((W_LORA, A_WIDTH), F32)
    w["wup"] = jnp.concatenate([P["rwkv_w_up"][l], zl], 0).astype(BF16)
    w["aup"] = jnp.concatenate([zl, P["rwkv_a_up"][l]], 0).astype(BF16)
    w["gup"] = P["rwkv_g_up"][l].astype(BF16)
    w["ln"] = jnp.pad(jnp.stack([P["rwkv_ln_w"][l], P["rwkv_ln_b"][l]]), ((0, 6), (0, 0)))
    for name, key in (("wba", "w_branch_a"), ("wbb", "w_branch_b"), ("wbc", "w_branch_c"),
                      ("wo", "w_mix_out"), ("f1i", "ffn1_w_in"), ("f1o", "ffn1_w_out"),
                      ("f2i", "ffn2_w_in"), ("f2o", "ffn2_w_out")):
        w[name] = P[key][l].astype(BF16)
    return w


def _rope_tables(pos):
    half = ROPE_DIM // 2
    inv = ROPE_BASE ** (-jnp.arange(half, dtype=F32) / half)
    ang = pos.astype(F32)[:, None] * inv[None, :]
    c, s = jnp.cos(ang), jnp.sin(ang)
    return jnp.concatenate([c, c], 1), jnp.concatenate([s, s], 1)


def _tab_prompt(seq):
    c, s = _rope_tables(jnp.arange(seq, dtype=jnp.int32))
    one = jnp.ones((seq, NOPE_DIM), F32)
    zn = jnp.zeros((seq, NOPE_DIM), F32)
    zr = jnp.zeros((seq, LANES - NOPE_DIM - ROPE_DIM), F32)
    return jnp.concatenate([one, c, zr, zn, s, zr, _pad_cols(c, LANES), _pad_cols(s, LANES)], axis=1)


def _tab_sample(n, dec_seq, past_len):
    pos = past_len + (jnp.arange(n, dtype=jnp.int32) % dec_seq)
    c, s = _rope_tables(pos)
    return jnp.concatenate([jnp.tile(c, (1, B_HEADS)), jnp.tile(s, (1, B_HEADS)),
                            _pad_cols(c, LANES), _pad_cols(s, LANES)], axis=1)


def _pad_state(s0):
    z = jnp.zeros_like(s0)
    even = jnp.concatenate([s0, z], -1)
    odd = jnp.concatenate([z, s0], -1)
    return jnp.where((jnp.arange(A_HEADS) % 2 == 0)[None, :, None, None], even, odd)


def _unpad_state(sp):
    even, odd = sp[..., :A_HEAD_DIM], sp[..., A_HEAD_DIM:]
    return jnp.where((jnp.arange(A_HEADS) % 2 == 0)[None, :, None, None], even, odd)


def _layer_prompt(x, w, tab, batch, seq, final_g):
    x = _ffn(x, w["n1"], w["f1i"], w["f1o"])
    sg, pa, qf, kcat, vv, lat, krope, sbk, sbv, sbqkv = _proj_prompt(x, w["nm"], w, tab, seq)
    sin, pin, vt = _rwkv_prep_prompt(pa, w, batch, seq)
    sa, yp, sfin = _rwkv_scan(sin, vt)
    ya = _rwkv_post(sa.reshape(batch * seq, A_WIDTH), yp.reshape(batch * seq, A_WIDTH),
                    pin.reshape(batch * seq, 5 * A_WIDTH), w["ln"])
    sfin = jnp.transpose(sfin.reshape(batch, A_HEADS // 2, A_HEAD_DIM, 2, A_HEAD_DIM),
                         (0, 1, 3, 2, 4)).reshape(batch, A_HEADS, A_HEAD_DIM, A_HEAD_DIM)
    yb = _mla_prompt(qf, kcat, vv, batch, seq).reshape(batch * seq, B_WIDTH)
    yc = _sb_prompt(sbqkv, batch, seq).reshape(batch * seq, C_WIDTH)
    x = _merge(x, sg, ya, yb, yc, w, absorbed=False, final_g=final_g)
    shift = pa.reshape(batch, seq, RWKV_COLS)[:, -1]
    ent = (lat.reshape(batch, seq, KV_RANK), krope.reshape(batch, seq, ROPE_DIM),
           sbk.reshape(batch, seq, C_HEADS, C_HEAD_DIM), sbv.reshape(batch, seq, C_HEADS, C_HEAD_DIM),
           sfin, shift)
    return x, ent


def _layer_sample(x, w, tab, l, page_table, caches, state, shift_prev, final_g):
    n = x.shape[0]
    x = _ffn(x, w["n1"], w["f1i"], w["f1o"])
    sg, pa, qlat, qrope, lat, krope, sbk, sbv, sbq = _proj_sample(x, w["nm"], w, tab)
    sin, pin = _rwkv_prep_sample(pa, shift_prev, w)
    sa, yp, sfin = _rwkv_step(sin, pin[:, :A_WIDTH], _pad_state(state))
    ya = _rwkv_post(sa, yp, pin, w["ln"])
    olat, osb = _decode_attn(l, page_table, qlat, qrope, sbq, lat, krope, *caches)
    x = _merge(x, sg, ya, olat, osb, w, absorbed=True, final_g=final_g)
    ent = (lat.reshape(n, 1, KV_RANK), krope.reshape(n, 1, ROPE_DIM),
           sbk.reshape(n, 1, C_HEADS, C_HEAD_DIM), sbv.reshape(n, 1, C_HEADS, C_HEAD_DIM),
           _unpad_state(sfin), pa)
    return x, ent


def kernel(x_prompt, x_sample, cache_mla_latent, cache_mla_krope, cache_sb_k, cache_sb_v,
           state_rwkv, state_rwkv_shift, page_table,
           norm_ffn1, ffn1_w_in, ffn1_w_out, norm_mix, w_mix_in,
           rwkv_mu, rwkv_w0, rwkv_w_up, rwkv_a0, rwkv_a_up, rwkv_g_up, rwkv_k_k, rwkv_k_a,
           rwkv_r_k, rwkv_ln_w, rwkv_ln_b,
           mla_q_norm, mla_w_uq, mla_kv_norm, mla_w_uk, mla_w_uv,
           w_branch_a, w_branch_b, w_branch_c, w_mix_out,
           norm_ffn2, ffn2_w_in, ffn2_w_out, final_norm):
    P = dict(ffn1_w_in=ffn1_w_in, ffn1_w_out=ffn1_w_out, w_mix_in=w_mix_in, rwkv_mu=rwkv_mu,
             rwkv_w0=rwkv_w0, rwkv_w_up=rwkv_w_up, rwkv_a0=rwkv_a0, rwkv_a_up=rwkv_a_up,
             rwkv_g_up=rwkv_g_up, rwkv_k_k=rwkv_k_k, rwkv_k_a=rwkv_k_a, rwkv_r_k=rwkv_r_k,
             rwkv_ln_w=rwkv_ln_w, rwkv_ln_b=rwkv_ln_b, mla_q_norm=mla_q_norm, mla_w_uq=mla_w_uq,
             mla_kv_norm=mla_kv_norm, mla_w_uk=mla_w_uk, mla_w_uv=mla_w_uv, w_branch_a=w_branch_a,
             w_branch_b=w_branch_b, w_branch_c=w_branch_c, w_mix_out=w_mix_out,
             ffn2_w_in=ffn2_w_in, ffn2_w_out=ffn2_w_out)
    depth = norm_ffn1.shape[0]
    bp, tp, d = x_prompt.shape
    n_dec, dec_seq, _ = x_sample.shape
    assert dec_seq == 1, "the sample group carries one new token per sequence"
    past_len = page_table.shape[1] * PAGE_SIZE
    tab_p = _tab_prompt(tp)
    tab_s = _tab_sample(n_dec * dec_seq, dec_seq, past_len)
    xp = x_prompt.reshape(bp * tp, d)
    xs = x_sample.reshape(n_dec * dec_seq, d)
    caches = (cache_mla_latent, cache_mla_krope, cache_sb_k, cache_sb_v)
    new_p, new_s = [], []
    for l in range(depth):
        w = _prep_weights(l, P)
        w["n1"], w["nm"], w["n2"] = norm_ffn1[l], norm_mix[l], norm_ffn2[l]
        final_g = final_norm if l == depth - 1 else None
        xp, ent_p = _layer_prompt(xp, w, tab_p, bp, tp, final_g)
        xs, ent_s = _layer_sample(xs, w, tab_s, l, page_table, caches, state_rwkv[l],
                                  state_rwkv_shift[l], final_g)
        new_p.append(ent_p)
        new_s.append(ent_s)
    outs_p = [jnp.stack(t) for t in zip(*new_p)]
    outs_s = [jnp.stack(t) for t in zip(*new_s)]
    return (xp.reshape(bp, tp, d), xs.reshape(n_dec, dec_seq, d), *outs_p, *outs_s)
```
